```python
import jax, jax.numpy as jnp
from jax import lax
import numpy as np

D_MODEL = 2048
BATCH = 4
SEQ = 2048
DEPTH = 2
DEC_BATCH = 8
DEC_SEQ = 8
PAST_LEN = 16384
PAGE_SIZE = 128

N_META = 16
N_EVEN = (DEPTH + 1) // 2
N_ODD = DEPTH // 2
MIX_WIDTH = D_MODEL
EPS = 1e-6
POOL_WINDOWS = (2, 4, 8, 16)
N_POOL_GROUPS = len(POOL_WINDOWS)
D_POOL = MIX_WIDTH // 2
POOL_GROUP = D_POOL // N_POOL_GROUPS
POOL_BUF = max(POOL_WINDOWS) - 1
GLA_HEADS = 4
GLA_DV = (MIX_WIDTH - D_POOL) // GLA_HEADS
GLA_DK = GLA_DV // 2
GLA_RANK = 16
GLA_NORMALIZER = 16.0
GLA_CHUNK = 64
FOX_HEADS = 16
FOX_HD = MIX_WIDTH // FOX_HEADS
Q_BLOCK = 128
D_FF = 4 * D_MODEL
_HK = GLA_HEADS * GLA_DK
_HV = GLA_HEADS * GLA_DV
IN_E = D_POOL + 2 * _HK + 2 * _HV + GLA_RANK
SPLIT_E = (D_POOL, D_POOL + _HK, D_POOL + 2 * _HK, D_POOL + 2 * _HK + _HV, D_POOL + 2 * _HK + 2 * _HV)
IN_O = 3 * MIX_WIDTH + FOX_HEADS
SPLIT_O = (MIX_WIDTH, 2 * MIX_WIDTH, 3 * MIX_WIDTH)

kernel_name = "hybrid_pool_gla_fox_decode_step"


def rms_norm(x, g):
    xf = x.astype(jnp.float32)
    y = xf * lax.rsqrt(jnp.mean(xf * xf, axis=-1, keepdims=True) + EPS)
    return (y * g.astype(jnp.float32)).astype(x.dtype)


def sq_relu_mlp(h, w_up, w_down):
    return jnp.square(jax.nn.relu(h @ w_up)) @ w_down


def pool_mix(u, buf, n_prev, pool_w, pool_scale):
    B, T, _ = u.shape
    ext = jnp.concatenate([buf.astype(u.dtype), u], axis=1).astype(jnp.float32)
    cs = jnp.pad(jnp.cumsum(ext, axis=1), ((0, 0), (1, 0), (0, 0)))
    upto = cs[:, POOL_BUF + 1:]
    uf = ext[:, POOL_BUF:]
    t = jnp.arange(T)
    groups = []
    for g, w in enumerate(POOL_WINDOWS):
        sl = slice(g * POOL_GROUP, (g + 1) * POOL_GROUP)
        before = cs[:, POOL_BUF + 1 - w: POOL_BUF + 1 - w + T, sl]
        cnt = jnp.minimum(w, n_prev + t + 1).astype(jnp.float32)[None, :, None]
        groups.append((upto[..., sl] - before) / cnt - uf[..., sl])
    pooled = jnp.stack(groups, axis=2).astype(u.dtype)
    mixed = jnp.einsum('btgc,gcd->btgd', pooled, pool_w).reshape(B, T, D_POOL)
    return mixed * pool_scale, ext[:, -POOL_BUF:].astype(u.dtype)


def gla_chunked(q, k, v, log_a, s0, front_pad):
    B, T, H, _ = q.shape
    total = front_pad + T
    back = (-total) % GLA_CHUNK
    n_chunks = (total + back) // GLA_CHUNK

    def chunks(a):
        a = jnp.pad(a.astype(jnp.float32), ((0, 0), (front_pad, back), (0, 0), (0, 0)))
        return a.reshape(B, n_chunks, GLA_CHUNK, H, a.shape[-1]).swapaxes(0, 1)

    qc, kc, vc, ac = chunks(q), chunks(k), chunks(v), chunks(log_a)
    causal = jnp.tril(jnp.ones((GLA_CHUNK, GLA_CHUNK), dtype=bool))[None, :, :, None, None]

    def step(S, inp):
        qi, ki, vi, ai = inp
        b = jnp.cumsum(ai, axis=1)
        inter = jnp.einsum('bthk,bhkv->bthv', qi * jnp.exp(b), S)
        decay = jnp.exp(jnp.where(causal, b[:, :, None] - b[:, None, :], -jnp.inf))
        att = jnp.einsum('bthk,bshk,btshk->bhts', qi, ki, decay)
        intra = jnp.einsum('bhts,bshv->bthv', att, vi)
        b_end = b[:, -1]
        S = S * jnp.exp(b_end)[..., None] + jnp.einsum('bshk,bshv->bhkv', ki * jnp.exp(b_end[:, None] - b), vi)
        return S, inter + intra

    s_fin, out = lax.scan(step, s0, (qc, kc, vc, ac))
    out = out.swapaxes(0, 1).reshape(B, n_chunks * GLA_CHUNK, H, -1)[:, front_pad:front_pad + T]
    return out, s_fin


def even_mixer(h, pool_buf, gla_s0, n_prev, front_pad, w_in, pool_w, pool_scale, gla_w_a2, gla_b_a, gla_norm, w_out):
    B, T, _ = h.shape
    z = h @ w_in
    u, q, k, v, r, a_lr = jnp.split(z, SPLIT_E, axis=-1)
    pool_out, new_buf = pool_mix(u, pool_buf, n_prev, pool_w, pool_scale)
    q = q.reshape(B, T, GLA_HEADS, GLA_DK) * (GLA_DK ** -0.5)
    k = k.reshape(B, T, GLA_HEADS, GLA_DK)
    v = v.reshape(B, T, GLA_HEADS, GLA_DV)
    log_a = (jax.nn.log_sigmoid((a_lr @ gla_w_a2 + gla_b_a).astype(jnp.float32)) / GLA_NORMALIZER)
    log_a = log_a.reshape(B, T, GLA_HEADS, GLA_DK)
    o, new_s = gla_chunked(q, k, v, log_a, gla_s0.astype(jnp.float32), front_pad)
    o = rms_norm(o.astype(h.dtype), gla_norm) * jax.nn.silu(r.reshape(B, T, GLA_HEADS, GLA_DV))
    mixed = jnp.concatenate([pool_out, o.reshape(B, T, _HV)], axis=-1)
    return mixed @ w_out, new_buf, new_s


def fox_project(h, w_in, f_bias):
    B, T, _ = h.shape
    q, k, v, fl = jnp.split(h @ w_in, SPLIT_O, axis=-1)
    shp = (B, T, FOX_HEADS, FOX_HD)
    log_f = jax.nn.log_sigmoid((fl + f_bias).astype(jnp.float32))
    return q.reshape(shp), k.reshape(shp), v.reshape(shp), log_f


def fox_prompt(q, k, v, log_f):
    B, T, H, Dh = q.shape
    n_blk = -(-T // Q_BLOCK)
    t_pad = n_blk * Q_BLOCK
    qp = jnp.pad(q, ((0, 0), (0, t_pad - T), (0, 0), (0, 0)))
    FT = jnp.cumsum(log_f, axis=1).transpose(0, 2, 1)
    FqT = jnp.pad(FT, ((0, 0), (0, 0), (0, t_pad - T)))
    kpos = jnp.arange(T)
    scale = FOX_HD ** -0.5

    def block(i):
        q_i = lax.dynamic_slice_in_dim(qp, i * Q_BLOCK, Q_BLOCK, axis=1)
        F_i = lax.dynamic_slice_in_dim(FqT, i * Q_BLOCK, Q_BLOCK, axis=2)
        qpos = i * Q_BLOCK + jnp.arange(Q_BLOCK)
        s = jnp.einsum('bqhd,bkhd->bhqk', q_i, k, preferred_element_type=jnp.float32) * scale
        s = s + F_i[..., :, None] - FT[..., None, :]
        s = jnp.where(qpos[:, None] >= kpos[None, :], s, -jnp.inf)
        p = jax.nn.softmax(s, axis=-1)
        return jnp.einsum('bhqk,bkhd->bqhd', p.astype(v.dtype), v)

    out = lax.map(block, jnp.arange(n_blk))
    return out.transpose(1, 0, 2, 3, 4).reshape(B, t_pad, H, Dh)[:, :T]


def fox_sample(q, k, v, log_f, k_cache, v_cache, lf_cache, page_table):
    Bd, Q, H, Dh = q.shape
    k_past = k_cache[page_table].reshape(Bd, -1, H, Dh)
    v_past = v_cache[page_table].reshape(Bd, -1, H, Dh)
    lf_past = lf_cache[page_table].reshape(Bd, -1, H).astype(jnp.float32)
    rc = lax.cumsum(lf_past, axis=1, reverse=True)
    G = jnp.concatenate([rc[:, 1:], jnp.zeros_like(rc[:, :1])], axis=1).transpose(0, 2, 1)
    L = jnp.cumsum(log_f, axis=1).transpose(0, 2, 1)
    scale = FOX_HD ** -0.5
    s_past = jnp.einsum('bqhd,bkhd->bhqk', q, k_past, preferred_element_type=jnp.float32) * scale
    s_past = s_past + L[..., :, None] + G[..., None, :]
    s_new = jnp.einsum('bqhd,bkhd->bhqk', q, k, preferred_element_type=jnp.float32) * scale
    s_new = s_new + L[..., :, None] - L[..., None, :]
    s_new = jnp.where(jnp.tril(jnp.ones((Q, Q), dtype=bool)), s_new, -jnp.inf)
    p = jax.nn.softmax(jnp.concatenate([s_past, s_new], axis=-1), axis=-1).astype(v.dtype)
    n_past = k_past.shape[1]
    return (jnp.einsum('bhqk,bkhd->bqhd', p[..., :n_past], v_past)
            + jnp.einsum('bhqk,bkhd->bqhd', p[..., n_past:], v))


def setup_inputs(seed: int = 0) -> dict:
    key = jax.random.key(seed)
    ks = jax.random.split(key, 25)
    f32 = jnp.float32
    nrm = lambda k, shape, s=1.0: jax.random.normal(k, shape, f32) * s
    gain = lambda k, shape: 1.0 + 0.05 * jax.random.normal(k, shape, f32)
    n_pages = PAST_LEN // PAGE_SIZE
    n_used = DEC_BATCH * n_pages
    n_pool = n_used + (n_used + 3) // 4
    page_table = jax.random.permutation(ks[7], n_pool)[:n_used].reshape(DEC_BATCH, n_pages).astype(jnp.int32)
    return {
        "x_prompt": nrm(ks[0], (BATCH, SEQ, D_MODEL)),
        "x_sample": nrm(ks[1], (DEC_BATCH, DEC_SEQ, D_MODEL)),
        "state_pool": nrm(ks[2], (N_EVEN, DEC_BATCH, POOL_BUF, D_POOL)),
        "state_gla": nrm(ks[3], (N_EVEN, DEC_BATCH, GLA_HEADS, GLA_DK, GLA_DV)),
        "cache_k": nrm(ks[4], (N_ODD, n_pool, PAGE_SIZE, FOX_HEADS, FOX_HD)),
        "cache_v": nrm(ks[5], (N_ODD, n_pool, PAGE_SIZE, FOX_HEADS, FOX_HD)),
        "cache_logf": jax.nn.log_sigmoid(8.0 + nrm(ks[6], (N_ODD, n_pool, PAGE_SIZE, FOX_HEADS), 0.5)),
        "page_table": page_table,
        "meta_tokens": nrm(ks[8], (N_META, D_MODEL)),
        "norm_mix_e": gain(ks[9], (N_EVEN, D_MODEL)),
        "w_in_e": nrm(ks[10], (N_EVEN, D_MODEL, IN_E), D_MODEL ** -0.5),
        "pool_w": nrm(ks[11], (N_EVEN, N_POOL_GROUPS, POOL_GROUP, POOL_GROUP), POOL_GROUP ** -0.5),
        "pool_scale": gain(ks[12], (N_EVEN, D_POOL)),
        "gla_w_a2": nrm(ks[13], (N_EVEN, GLA_RANK, _HK), GLA_RANK ** -0.5),
        "gla_b_a": nrm(ks[14], (N_EVEN, _HK), 0.1),
        "gla_norm": gain(ks[15], (N_EVEN, GLA_HEADS, GLA_DV)),
        "w_out_e": nrm(ks[16], (N_EVEN, MIX_WIDTH, D_MODEL), MIX_WIDTH ** -0.5),
        "norm_mix_o": gain(ks[17], (N_ODD, D_MODEL)),
        "w_in_o": nrm(ks[18], (N_ODD, D_MODEL, IN_O), D_MODEL ** -0.5),
        "fox_f_bias": jax.random.uniform(ks[19], (N_ODD, FOX_HEADS), f32, 3.0, 8.0),
        "w_out_o": nrm(ks[20], (N_ODD, MIX_WIDTH, D_MODEL), MIX_WIDTH ** -0.5),
        "norm_mlp": gain(ks[21], (DEPTH, D_MODEL)),
        "w_up": nrm(ks[22], (DEPTH, D_MODEL, D_FF), D_MODEL ** -0.5),
        "w_down": nrm(ks[23], (DEPTH, D_FF, D_MODEL), D_FF ** -0.5),
        "norm_final": gain(ks[24], (D_MODEL,)),
    }


def reference(x_prompt, x_sample, state_pool, state_gla, cache_k, cache_v, cache_logf, page_table,
              meta_tokens, norm_mix_e, w_in_e, pool_w, pool_scale, gla_w_a2, gla_b_a, gla_norm, w_out_e,
              norm_mix_o, w_in_o, fox_f_bias, w_out_o, norm_mlp, w_up, w_down, norm_final):
    Bp = x_prompt.shape[0]
    n_past = page_table.shape[1] * PAGE_SIZE
    meta = jnp.broadcast_to(meta_tokens[None].astype(x_prompt.dtype), (Bp, N_META, D_MODEL))
    hp = jnp.concatenate([meta, x_prompt], axis=1)
    hs = x_sample
    pool_p, pool_s, gla_p, gla_s = [], [], [], []
    k_p, k_s, v_p, v_s, lf_p, lf_s = [], [], [], [], [], []
    for l in range(DEPTH):
        i = l // 2
        if l % 2 == 0:
            prm = (w_in_e[i], pool_w[i], pool_scale[i], gla_w_a2[i], gla_b_a[i], gla_norm[i], w_out_e[i])
            zb = jnp.zeros((Bp, POOL_BUF, D_POOL), hp.dtype)
            zs = jnp.zeros((Bp, GLA_HEADS, GLA_DK, GLA_DV), jnp.float32)
            yp, bp, sp = even_mixer(rms_norm(hp, norm_mix_e[i]), zb, zs, 0, (-N_META) % GLA_CHUNK, *prm)
            ys, bs, ss = even_mixer(rms_norm(hs, norm_mix_e[i]), state_pool[i], state_gla[i], n_past, 0, *prm)
            pool_p.append(bp); pool_s.append(bs); gla_p.append(sp); gla_s.append(ss)
        else:
            qp, kp, vp, lfp = fox_project(rms_norm(hp, norm_mix_o[i]), w_in_o[i], fox_f_bias[i])
            qs, ks_, vs, lfs = fox_project(rms_norm(hs, norm_mix_o[i]), w_in_o[i], fox_f_bias[i])
            op = fox_prompt(qp, kp, vp, lfp)
            os_ = fox_sample(qs, ks_, vs, lfs, cache_k[i], cache_v[i], cache_logf[i], page_table)
            yp = op.reshape(op.shape[0], op.shape[1], MIX_WIDTH) @ w_out_o[i]
            ys = os_.reshape(os_.shape[0], os_.shape[1], MIX_WIDTH) @ w_out_o[i]
            k_p.append(kp); k_s.append(ks_); v_p.append(vp); v_s.append(vs)
            lf_p.append(lfp.astype(cache_logf.dtype)); lf_s.append(lfs.astype(cache_logf.dtype))
        hp = hp + yp
        hs = hs + ys
        hp = hp + sq_relu_mlp(rms_norm(hp, norm_mlp[l]), w_up[l], w_down[l])
        hs = hs + sq_relu_mlp(rms_norm(hs, norm_mlp[l]), w_up[l], w_down[l])
    y_prompt = rms_norm(hp, norm_final)[:, N_META:]
    y_sample = rms_norm(hs, norm_final)
    return (y_prompt, y_sample,
            jnp.stack(pool_p), jnp.stack(pool_s), jnp.stack(gla_p), jnp.stack(gla_s),
            jnp.stack(k_p), jnp.stack(k_s), jnp.stack(v_p), jnp.stack(v_s),
            jnp.stack(lf_p), jnp.stack(lf_s))
```

```python
import functools

import jax
import jax.numpy as jnp
from jax import lax
from jax.experimental import pallas as pl
from jax.experimental.pallas import tpu as pltpu

F32 = jnp.float32
BF16 = jnp.bfloat16
HIGHEST = lax.Precision.HIGHEST

D_MODEL = 2048
N_META = 16
EPS = 1e-6
POOL_WINDOWS = (2, 4, 8, 16)
D_POOL = 1024
POOL_GROUP = 256
POOL_BUF = 15
GLA_HEADS = 4
GLA_DK = 128
GLA_DV = 256
GLA_RANK = 16
GLA_NORMALIZER = 16.0
FOX_HEADS = 16
FOX_HD = 128
PAGE_SIZE = 128
D_FF = 8192
HK = GLA_HEADS * GLA_DK
HV = GLA_HEADS * GLA_DV

LANES = 128
NEG_BIG = -1e30
VMEM_LIMIT = 56 * 1024 * 1024

ROW_BLOCK = 1032
SEQ_BLOCK = 688
GLA_CHUNK = 48
PAGES_PER_STEP = 4

NT_DIMS = (((1,), (1,)), ((), ()))
TN_DIMS = (((0,), (0,)), ((), ()))


def _params(semantics):
    return pltpu.CompilerParams(dimension_semantics=semantics, vmem_limit_bytes=VMEM_LIMIT)


def _dot(a, b, **kw):
    return jnp.dot(a, b, preferred_element_type=F32, **kw)


def _dot_nt(a, b, **kw):
    return lax.dot_general(a, b, NT_DIMS, preferred_element_type=F32, **kw)


def _dot_tn(a, b, **kw):
    return lax.dot_general(a, b, TN_DIMS, preferred_element_type=F32, **kw)


def _log_sigmoid(x):
    return jnp.minimum(x, 0.0) - jnp.log1p(jnp.exp(-jnp.abs(x)))


def _rms_norm(x, g):
    ms = jnp.mean(x * x, axis=-1, keepdims=True)
    return x * lax.rsqrt(ms + EPS) * g


def _in_proj_kernel(sections, logsig_extra, xp_ref, xs_ref, g_ref, w_ref, wx_ref, bx_ref, *refs):
    nsec = len(sections)
    outs_p, zxp_ref = refs[:nsec], refs[nsec]
    outs_s, zxs_ref = refs[nsec + 1:2 * nsec + 1], refs[2 * nsec + 1]
    xnp_ref, xns_ref = refs[2 * nsec + 2:]
    i = pl.program_id(0)
    j = pl.program_id(1)

    def extra(xn):
        e = _dot(xn, wx_ref[...]) + bx_ref[...]
        return _log_sigmoid(e) if logsig_extra else e

    @pl.when(j == 0)
    def _():
        xn = _rms_norm(xp_ref[...], g_ref[...]).astype(BF16)
        xnp_ref[...] = xn
        zxp_ref[...] = extra(xn)

    @pl.when((j == 0) & (i == 0))
    def _():
        xn = _rms_norm(xs_ref[...], g_ref[...]).astype(BF16)
        xns_ref[...] = xn
        zxs_ref[...] = extra(xn)

    def write(outs, x_ref):
        z = _dot(x_ref[...], w_ref[...])
        off = 0
        for o_ref, n in zip(outs, sections):
            if nsec == 1:
                o_ref[...] = z.astype(o_ref.dtype)
            else:
                @pl.when((j >= off) & (j < off + n))
                def _(o_ref=o_ref):
                    o_ref[...] = z.astype(o_ref.dtype)
            off += n

    write(outs_p, xnp_ref)

    @pl.when(i == 0)
    def _():
        write(outs_s, xns_ref)


def _in_proj(xp, xs, g, w, wx, bx, sections, tn, logsig_extra):
    n_p, n_s = xp.shape[0], xs.shape[0]
    ncol = sum(sections)
    grid = (n_p // ROW_BLOCK, ncol)
    offs = [sum(sections[:k]) for k in range(len(sections))]

    def p_map(off, n):
        return lambda i, j: (i, jnp.clip(j - off, 0, n - 1))

    def s_map(off, n):
        return lambda i, j: (0, jnp.clip(jnp.where(i == 0, j, ncol - 1) - off, 0, n - 1))

    out_shape, out_specs = [], []
    for rows, tm, mk in ((n_p, ROW_BLOCK, p_map), (n_s, n_s, s_map)):
        for off, n in zip(offs, sections):
            out_shape.append(jax.ShapeDtypeStruct((rows, n * tn), F32))
            out_specs.append(pl.BlockSpec((tm, tn), mk(off, n)))
        out_shape.append(jax.ShapeDtypeStruct((rows, LANES), F32))
        out_specs.append(pl.BlockSpec((tm, LANES), (lambda i, j: (i, 0)) if rows == n_p else (lambda i, j: (0, 0))))
    outs = pl.pallas_call(
        functools.partial(_in_proj_kernel, tuple(sections), logsig_extra),
        grid=grid,
        in_specs=[
            pl.BlockSpec((ROW_BLOCK, D_MODEL), lambda i, j: (i, 0)),
            pl.BlockSpec((n_s, D_MODEL), lambda i, j: (0, 0)),
            pl.BlockSpec((1, D_MODEL), lambda i, j: (0, 0)),
            pl.BlockSpec((D_MODEL, tn), lambda i, j: (0, j)),
            pl.BlockSpec((D_MODEL, LANES), lambda i, j: (0, 0)),
            pl.BlockSpec((1, LANES), lambda i, j: (0, 0)),
        ],
        out_specs=out_specs,
        out_shape=out_shape,
        scratch_shapes=[pltpu.VMEM((ROW_BLOCK, D_MODEL), BF16), pltpu.VMEM((n_s, D_MODEL), BF16)],
        compiler_params=_params(("arbitrary", "arbitrary")),
        name="in_proj",
    )(xp, xs, g, w, wx, bx)
    k = len(sections) + 1
    return outs[:k], outs[k:]


def _out_proj_kernel(widths, xp_ref, xs_ref, w_ref, *refs):
    na = len(widths)
    ap, a_s = refs[:na], refs[na:2 * na]
    op_ref, os_ref = refs[2 * na:]
    i = pl.program_id(0)

    def compute(x_ref, a_refs, o_ref):
        acc = x_ref[...]
        off = 0
        for a_ref, kw in zip(a_refs, widths):
            acc = acc + _dot(a_ref[...].astype(BF16), w_ref[off:off + kw, :])
            off += kw
        o_ref[...] = acc

    compute(xp_ref, ap, op_ref)

    @pl.when(i == 0)
    def _():
        compute(xs_ref, a_s, os_ref)


def _out_proj(xp, xs, w, a_p, a_s, tn=1024):
    n_p, n_s = xp.shape[0], xs.shape[0]
    widths = tuple(a.shape[1] for a in a_p)
    ncol = D_MODEL // tn
    grid = (n_p // ROW_BLOCK, ncol)
    s_col = lambda i, j: (0, jnp.where(i == 0, j, ncol - 1))
    in_specs = [
        pl.BlockSpec((ROW_BLOCK, tn), lambda i, j: (i, j)),
        pl.BlockSpec((n_s, tn), s_col),
        pl.BlockSpec((D_MODEL, tn), lambda i, j: (0, j)),
    ]
    in_specs += [pl.BlockSpec((ROW_BLOCK, kw), lambda i, j: (i, 0)) for kw in widths]
    in_specs += [pl.BlockSpec((n_s, kw), lambda i, j: (0, 0)) for kw in widths]
    return pl.pallas_call(
        functools.partial(_out_proj_kernel, widths),
        grid=grid,
        in_specs=in_specs,
        out_specs=[pl.BlockSpec((ROW_BLOCK, tn), lambda i, j: (i, j)), pl.BlockSpec((n_s, tn), s_col)],
        out_shape=[jax.ShapeDtypeStruct((n_p, D_MODEL), F32), jax.ShapeDtypeStruct((n_s, D_MODEL), F32)],
        compiler_params=_params(("arbitrary", "arbitrary")),
        name="out_proj",
    )(xp, xs, w, *a_p, *a_s)


def _mlp_kernel(final_norm, xp_ref, xs_ref, g_ref, gf_ref, wu_ref, wd_ref, op_ref, os_ref, xnp_ref, xns_ref):
    i = pl.program_id(0)
    j = pl.program_id(1)
    last = pl.num_programs(1) - 1

    def step(x_ref, xn_ref, o_ref):
        @pl.when(j == 0)
        def _():
            x = x_ref[...]
            xn_ref[...] = _rms_norm(x, g_ref[...]).astype(BF16)
            o_ref[...] = x

        a = _dot(xn_ref[...], wu_ref[...])
        a = jnp.square(jnp.maximum(a, 0.0)).astype(BF16)
        o_ref[...] += _dot(a, wd_ref[...])
        if final_norm:
            @pl.when(j == last)
            def _():
                o_ref[...] = _rms_norm(o_ref[...], gf_ref[...])

    step(xp_ref, xnp_ref, op_ref)

    @pl.when(i == 0)
    def _():
        step(xs_ref, xns_ref, os_ref)


def _mlp(xp, xs, g, gf, wu, wd, final_norm, tf=512):
    n_p, n_s = xp.shape[0], xs.shape[0]
    grid = (n_p // ROW_BLOCK, D_FF // tf)
    return pl.pallas_call(
        functools.partial(_mlp_kernel, final_norm),
        grid=grid,
        in_specs=[
            pl.BlockSpec((ROW_BLOCK, D_MODEL), lambda i, j: (i, 0)),
            pl.BlockSpec((n_s, D_MODEL), lambda i, j: (0, 0)),
            pl.BlockSpec((1, D_MODEL), lambda i, j: (0, 0)),
            pl.BlockSpec((1, D_MODEL), lambda i, j: (0, 0)),
            pl.BlockSpec((D_MODEL, tf), lambda i, j: (0, j)),
            pl.BlockSpec((tf, D_MODEL), lambda i, j: (j, 0)),
        ],
        out_specs=[pl.BlockSpec((ROW_BLOCK, D_MODEL), lambda i, j: (i, 0)),
                   pl.BlockSpec((n_s, D_MODEL), lambda i, j: (0, 0))],
        out_shape=[jax.ShapeDtypeStruct((n_p, D_MODEL), F32), jax.ShapeDtypeStruct((n_s, D_MODEL), F32)],
        scratch_shapes=[pltpu.VMEM((ROW_BLOCK, D_MODEL), BF16), pltpu.VMEM((n_s, D_MODEL), BF16)],
        compiler_params=_params(("arbitrary", "arbitrary")),
        name="mlp",
    )(xp, xs, g, gf, wu, wd)


def _pool_groups(ext_ref, rows, count_fn, pw_ref, ps_ref, o_ref):
    for g, w in enumerate(POOL_WINDOWS):
        cs = slice(g * POOL_GROUP, (g + 1) * POOL_GROUP)
        u = ext_ref[16:16 + rows, cs]
        acc = u
        for back in range(1, w):
            acc = acc + ext_ref[16 - back:16 - back + rows, cs]
        pooled = acc / count_fn(w) - u
        mixed = _dot(pooled.astype(BF16), pw_ref[g]) * ps_ref[:, cs]
        o_ref[:, cs] = mixed.astype(o_ref.dtype)


def _pool_prompt_kernel(z_ref, pw_ref, ps_ref, o_ref, buf_ref, ext_ref):
    blk = pl.program_id(0) % (2064 // SEQ_BLOCK)

    @pl.when(blk == 0)
    def _():
        ext_ref[0:16, :] = jnp.zeros((16, D_POOL), F32)

    ext_ref[16:16 + SEQ_BLOCK, :] = z_ref[...]
    pos = blk * SEQ_BLOCK + lax.broadcasted_iota(jnp.int32, (SEQ_BLOCK, 1), 0)
    _pool_groups(ext_ref, SEQ_BLOCK, lambda w: jnp.minimum(w, pos + 1).astype(F32), pw_ref, ps_ref, o_ref)

    @pl.when(blk == 2064 // SEQ_BLOCK - 1)
    def _():
        buf_ref[...] = ext_ref[16 + SEQ_BLOCK - POOL_BUF:16 + SEQ_BLOCK, :]

    ext_ref[0:16, :] = ext_ref[SEQ_BLOCK:SEQ_BLOCK + 16, :]


def _pool_prompt(z_main, pw, ps, n_batch):
    n_p = z_main.shape[0]
    per_seq = 2064 // SEQ_BLOCK
    return pl.pallas_call(
        _pool_prompt_kernel,
        grid=(n_p // SEQ_BLOCK,),
        in_specs=[
            pl.BlockSpec((SEQ_BLOCK, D_POOL), lambda r: (r, 0)),
            pl.BlockSpec((len(POOL_WINDOWS), POOL_GROUP, POOL_GROUP), lambda r: (0, 0, 0)),
            pl.BlockSpec((1, D_POOL), lambda r: (0, 0)),
        ],
        out_specs=[pl.BlockSpec((SEQ_BLOCK, D_POOL), lambda r: (r, 0)),
                   pl.BlockSpec((None, None, POOL_BUF, D_POOL), lambda r: (0, r // per_seq, 0, 0))],
        out_shape=[jax.ShapeDtypeStruct((n_p, D_POOL), BF16),
                   jax.ShapeDtypeStruct((1, n_batch, POOL_BUF, D_POOL), F32)],
        scratch_shapes=[pltpu.VMEM((16 + SEQ_BLOCK, D_POOL), F32)],
        compiler_params=_params(("arbitrary",)),
        name="pool_prompt",
    )(z_main, pw, ps)


def _gla_chunk(qk, v, r, alr, wa2, ba, gnorm, st_get, st_set, o_set):
    c = qk.shape[0]
    x = _dot(alr.astype(BF16), wa2) + ba
    la = _log_sigmoid(x) * (1.0 / GLA_NORMALIZER)
    ri = lax.broadcasted_iota(jnp.int32, (c, c), 0)
    ci = lax.broadcasted_iota(jnp.int32, (c, c), 1)
    causal = ri >= ci
    b = _dot(causal.astype(F32), la, precision=HIGHEST)
    mid = c // 2 - 1
    bm = b[mid:mid + 1, :]
    be = b[c - 1:c, :]
    q = qk[:, :HK] * (GLA_DK ** -0.5)
    k = qk[:, HK:]
    qt = q * jnp.exp(b - bm)
    kt = k * jnp.exp(bm - b)
    qi = qt * jnp.exp(bm)
    khat = kt * jnp.exp(be - bm)
    e_end = jnp.exp(be)
    for h in range(GLA_HEADS):
        ks = slice(h * GLA_DK, (h + 1) * GLA_DK)
        vs = slice(h * GLA_DV, (h + 1) * GLA_DV)
        vh = v[:, vs].astype(BF16)
        att = _dot_nt(qt[:, ks].astype(BF16), kt[:, ks].astype(BF16))
        att = jnp.where(causal, att, 0.0)
        st = st_get(h)
        o = _dot(att.astype(BF16), vh) + _dot_nt(qi[:, ks].astype(BF16), st.astype(BF16))
        st_set(h, st * e_end[:, ks] + _dot_tn(vh, khat[:, ks].astype(BF16)))
        on = _rms_norm(o, gnorm[:, vs])
        rh = r[:, vs]
        gate = rh / (1.0 + jnp.exp(-rh))
        o_set(h, on * gate)


def _gla_prompt_kernel(n_batch, qk_ref, v_ref, r_ref, alr_ref, wa2_ref, ba_ref, gn_ref, o_ref, s_ref, st_ref):
    c = pl.program_id(0)

    @pl.when(c == 0)
    def _():
        st_ref[...] = jnp.zeros(st_ref.shape, F32)

    for bi in range(n_batch):
        def st_get(h, bi=bi):
            return st_ref[bi * GLA_HEADS + h]

        def st_set(h, val, bi=bi):
            st_ref[bi * GLA_HEADS + h] = val

        def o_set(h, val, bi=bi):
            o_ref[bi, :, h * GLA_DV:(h + 1) * GLA_DV] = val.astype(o_ref.dtype)

        _gla_chunk(qk_ref[bi], v_ref[bi], r_ref[bi], alr_ref[bi], wa2_ref[...], ba_ref[...], gn_ref[...],
                   st_get, st_set, o_set)

    @pl.when(c == pl.num_programs(0) - 1)
    def _():
        for bi in range(n_batch):
            for h in range(GLA_HEADS):
                s_ref[0, bi, h] = st_ref[bi * GLA_HEADS + h].T


def _gla_prompt(z_main3, zx3, wa2, ba, gn):
    n_batch, seq = z_main3.shape[0], z_main3.shape[1]
    blk = lambda col: pl.BlockSpec((n_batch, GLA_CHUNK, 1024), lambda c, col=col: (0, c, col))
    const2 = lambda shape: pl.BlockSpec(shape, lambda c: (0, 0))
    return pl.pallas_call(
        functools.partial(_gla_prompt_kernel, n_batch),
        grid=(seq // GLA_CHUNK,),
        in_specs=[blk(1), blk(2), blk(3),
                  pl.BlockSpec((n_batch, GLA_CHUNK, LANES), lambda c: (0, c, 0)),
                  const2((LANES, HK)), const2((1, HK)), const2((1, HV))],
        out_specs=[pl.BlockSpec((n_batch, GLA_CHUNK, HV), lambda c: (0, c, 0)),
                   pl.BlockSpec((1, n_batch, GLA_HEADS, GLA_DK, GLA_DV), lambda c: (0, 0, 0, 0, 0))],
        out_shape=[jax.ShapeDtypeStruct((n_batch, seq, HV), BF16),
                   jax.ShapeDtypeStruct((1, n_batch, GLA_HEADS, GLA_DK, GLA_DV), F32)],
        scratch_shapes=[pltpu.VMEM((n_batch * GLA_HEADS, GLA_DV, GLA_DK), F32)],
        compiler_params=_params(("arbitrary",)),
        name="gla_prompt",
    )(z_main3, z_main3, z_main3, zx3, wa2, ba, gn)


def _even_sample_kernel(u_ref, qk_ref, v_ref, r_ref, alr_ref, pbuf_ref, s0_ref, pw_ref, ps_ref, wa2_ref, ba_ref,
                        gn_ref, po_ref, go_ref, nbuf_ref, ns_ref, ext_ref):
    t = u_ref.shape[0]
    ext_ref[0:1, :] = jnp.zeros((1, D_POOL), F32)
    ext_ref[1:16, :] = pbuf_ref[...]
    ext_ref[16:16 + t, :] = u_ref[...]
    _pool_groups(ext_ref, t, lambda w: float(w), pw_ref, ps_ref, po_ref)
    nbuf_ref[...] = ext_ref[16 + t - POOL_BUF:16 + t, :]

    def st_get(h):
        return s0_ref[h].T

    def st_set(h, val):
        ns_ref[h] = val.T

    def o_set(h, val):
        go_ref[:, h * GLA_DV:(h + 1) * GLA_DV] = val

    _gla_chunk(qk_ref[...], v_ref[...], r_ref[...], alr_ref[...], wa2_ref[...], ba_ref[...], gn_ref[...],
               st_get, st_set, o_set)


def _even_sample(zs_main, zxs, pool_buf, gla_s0, pw, ps, wa2, ba, gn, n_batch, t):
    blk = lambda col: pl.BlockSpec((t, 1024), lambda b, col=col: (b, col))
    const = lambda shape: pl.BlockSpec(shape, lambda b: (0,) * len(shape))
    return pl.pallas_call(
        _even_sample_kernel,
        grid=(n_batch,),
        in_specs=[blk(0), blk(1), blk(2), blk(3),
                  pl.BlockSpec((t, LANES), lambda b: (b, 0)),
                  pl.BlockSpec((None, None, POOL_BUF, D_POOL), lambda b: (0, b, 0, 0)),
                  pl.BlockSpec((None, None, GLA_HEADS, GLA_DK, GLA_DV), lambda b: (0, b, 0, 0, 0)),
                  const((len(POOL_WINDOWS), POOL_GROUP, POOL_GROUP)), const((1, D_POOL)),
                  const((LANES, HK)), const((1, HK)), const((1, HV))],
        out_specs=[pl.BlockSpec((t, D_POOL), lambda b: (b, 0)),
                   pl.BlockSpec((t, HV), lambda b: (b, 0)),
                   pl.BlockSpec((None, None, POOL_BUF, D_POOL), lambda b: (0, b, 0, 0)),
                   pl.BlockSpec((None, None, GLA_HEADS, GLA_DK, GLA_DV), lambda b: (0, b, 0, 0, 0))],
        out_shape=[jax.ShapeDtypeStruct((n_batch * t, D_POOL), F32),
                   jax.ShapeDtypeStruct((n_batch * t, HV), F32),
                   jax.ShapeDtypeStruct((1, n_batch, POOL_BUF, D_POOL), F32),
                   jax.ShapeDtypeStruct((1, n_batch, GLA_HEADS, GLA_DK, GLA_DV), F32)],
        scratch_shapes=[pltpu.VMEM((16 + t, D_POOL), F32)],
        compiler_params=_params(("arbitrary",)),
        name="even_sample",
    )(zs_main, zs_main, zs_main, zs_main, zxs, pool_buf, gla_s0, pw, ps, wa2, ba, gn)


def _forget_cumsum_kernel(lf_ref, f_ref, ft_ref, a_ref, b_ref):
    seq = lf_ref.shape[0]
    pad = a_ref.shape[0] - seq
    a_ref[0:pad, :] = jnp.zeros((pad, LANES), F32)
    b_ref[0:pad, :] = jnp.zeros((pad, LANES), F32)
    a_ref[pad:, :] = lf_ref[...]
    src, dst = a_ref, b_ref
    shift = 1
    while shift < seq:
        dst[pad:, :] = src[pad:, :] + src[pad - shift:pad - shift + seq, :]
        src, dst = dst, src
        shift *= 2
    f = src[pad:, :]
    f_ref[...] = f
    eye = (lax.broadcasted_iota(jnp.int32, (LANES, LANES), 0)
           == lax.broadcasted_iota(jnp.int32, (LANES, LANES), 1)).astype(F32)
    for kb in range(seq // SEQ_BLOCK):
        ft_ref[kb] = _dot_nt(eye, f[kb * SEQ_BLOCK:(kb + 1) * SEQ_BLOCK, :], precision=HIGHEST)


def _forget_cumsum(lf3):
    n_batch, seq = lf3.shape[0], lf3.shape[1]
    pad = 2048
    nkb = seq // SEQ_BLOCK
    return pl.pallas_call(
        _forget_cumsum_kernel,
        grid=(n_batch,),
        in_specs=[pl.BlockSpec((None, seq, LANES), lambda b: (b, 0, 0))],
        out_specs=[pl.BlockSpec((None, seq, LANES), lambda b: (b, 0, 0)),
                   pl.BlockSpec((None, nkb, LANES, SEQ_BLOCK), lambda b: (b, 0, 0, 0))],
        out_shape=[jax.ShapeDtypeStruct((n_batch, seq, LANES), F32),
                   jax.ShapeDtypeStruct((n_batch, nkb, LANES, SEQ_BLOCK), F32)],
        scratch_shapes=[pltpu.VMEM((pad + seq, LANES), F32), pltpu.VMEM((pad + seq, LANES), F32)],
        compiler_params=_params(("arbitrary",)),
        name="forget_cumsum",
    )(lf3)


HEADS_PER_STEP = 4


def _fox_prompt_kernel(q_ref, k_ref, v_ref, f_ref, ft_ref, o_ref, m_ref, l_ref, acc_ref, fcol_ref):
    hg = pl.program_id(1)
    qi = pl.program_id(2)
    ki = pl.program_id(3)
    tq = q_ref.shape[0]
    scale = FOX_HD ** -0.5

    @pl.when(ki == 0)
    def _():
        m_ref[...] = jnp.full(m_ref.shape, NEG_BIG, F32)
        l_ref[...] = jnp.zeros(l_ref.shape, F32)
        acc_ref[...] = jnp.zeros(acc_ref.shape, F32)
        lane = lax.broadcasted_iota(jnp.int32, (tq, LANES), 1)
        f = f_ref[...]
        for hh in range(HEADS_PER_STEP):
            fcol_ref[hh] = jnp.sum(jnp.where(lane == hg * HEADS_PER_STEP + hh, f, 0.0), axis=-1, keepdims=True)

    @pl.when(ki <= qi)
    def _():
        ri = lax.broadcasted_iota(jnp.int32, (tq, tq), 0)
        ci = lax.broadcasted_iota(jnp.int32, (tq, tq), 1)
        visible = (ci <= ri) | (ki < qi)
        for hh in range(HEADS_PER_STEP):
            cs = slice(hh * FOX_HD, (hh + 1) * FOX_HD)
            s = _dot_nt(q_ref[:, cs].astype(BF16), k_ref[:, cs].astype(BF16)) * scale
            s = s + fcol_ref[hh] - ft_ref[pl.ds(hg * HEADS_PER_STEP + hh, 1), :]
            s = jnp.where(visible, s, NEG_BIG)
            m_old = m_ref[hh]
            m_new = jnp.maximum(m_old, jnp.max(s, axis=-1, keepdims=True))
            alpha = jnp.exp(m_old - m_new)
            p = jnp.exp(s - m_new)
            l_ref[hh] = alpha * l_ref[hh] + jnp.sum(p, axis=-1, keepdims=True)
            acc_ref[:, cs] = alpha * acc_ref[:, cs] + _dot(p.astype(BF16), v_ref[:, cs].astype(BF16))
            m_ref[hh] = m_new

    @pl.when(ki == qi)
    def _():
        for hh in range(HEADS_PER_STEP):
            cs = slice(hh * FOX_HD, (hh + 1) * FOX_HD)
            o_ref[:, cs] = (acc_ref[:, cs] / l_ref[hh]).astype(o_ref.dtype)


def _fox_prompt(q3, k3, v3, f3, ft4):
    n_batch, seq = q3.shape[0], q3.shape[1]
    nblk = seq // SEQ_BLOCK
    wcol = HEADS_PER_STEP * FOX_HD
    kv_spec = pl.BlockSpec((None, SEQ_BLOCK, wcol), lambda b, hg, qi, ki: (b, jnp.minimum(ki, qi), hg))
    return pl.pallas_call(
        _fox_prompt_kernel,
        grid=(n_batch, FOX_HEADS // HEADS_PER_STEP, nblk, nblk),
        in_specs=[
            pl.BlockSpec((None, SEQ_BLOCK, wcol), lambda b, hg, qi, ki: (b, qi, hg)),
            kv_spec, kv_spec,
            pl.BlockSpec((None, SEQ_BLOCK, LANES), lambda b, hg, qi, ki: (b, qi, 0)),
            pl.BlockSpec((None, None, LANES, SEQ_BLOCK), lambda b, hg, qi, ki: (b, jnp.minimum(ki, qi), 0, 0)),
        ],
        out_specs=pl.BlockSpec((None, SEQ_BLOCK, wcol), lambda b, hg, qi, ki: (b, qi, hg)),
        out_shape=jax.ShapeDtypeStruct((n_batch, seq, D_MODEL), BF16),
        scratch_shapes=[pltpu.VMEM((HEADS_PER_STEP, SEQ_BLOCK, 1), F32),
                        pltpu.VMEM((HEADS_PER_STEP, SEQ_BLOCK, 1), F32),
                        pltpu.VMEM((SEQ_BLOCK, wcol), F32),
                        pltpu.VMEM((HEADS_PER_STEP, SEQ_BLOCK, 1), F32)],
        compiler_params=_params(("arbitrary", "arbitrary", "arbitrary", "arbitrary")),
        name="fox_prompt",
    )(q3, k3, v3, f3, ft4)


def _fox_sample_kernel(pt_ref, qs_ref, kn_ref, vn_ref, lfn_ref, *refs):
    g_pages = PAGES_PER_STEP
    k_pages = refs[:g_pages]
    v_pages = refs[g_pages:2 * g_pages]
    lf_pages = refs[2 * g_pages:3 * g_pages]
    o_ref = refs[3 * g_pages]
    qbd_ref, acc_ref, m_ref, l_ref, lcol_ref, carry_ref, kbuf_ref, vbuf_ref, lfw_ref = refs[3 * g_pages + 1:]
    j = pl.program_id(1)
    t = qs_ref.shape[0]
    rows = FOX_HEADS * t
    scale = FOX_HD ** -0.5
    row = lax.broadcasted_iota(jnp.int32, (rows, LANES), 0)
    lane = lax.broadcasted_iota(jnp.int32, (rows, LANES), 1)
    head_sel = (row // t == lane).astype(F32)

    def online_update(s, v_bf16):
        m_old = m_ref[...]
        m_new = jnp.maximum(m_old, jnp.max(s, axis=-1, keepdims=True))
        alpha = jnp.exp(m_old - m_new)
        p = jnp.exp(s - m_new)
        l_ref[...] = alpha * l_ref[...] + jnp.sum(p, axis=-1, keepdims=True)
        acc_ref[...] = alpha * acc_ref[...] + _dot(p.astype(BF16), v_bf16)
        m_ref[...] = m_new

    @pl.when(j == 0)
    def _():
        m_ref[...] = jnp.full(m_ref.shape, NEG_BIG, F32)
        l_ref[...] = jnp.zeros(l_ref.shape, F32)
        acc_ref[...] = jnp.zeros(acc_ref.shape, F32)
        carry_ref[...] = jnp.zeros(carry_ref.shape, F32)
        lfw_ref[...] = jnp.zeros(lfw_ref.shape, F32)
        q_rep = jnp.tile(qs_ref[...], (FOX_HEADS, 1))
        r2 = lax.broadcasted_iota(jnp.int32, (rows, D_MODEL), 0)
        c2 = lax.broadcasted_iota(jnp.int32, (rows, D_MODEL), 1)
        qbd_ref[...] = jnp.where(r2 // t == c2 // FOX_HD, q_rep, 0.0).astype(BF16)
        lfn = jnp.where(lax.broadcasted_iota(jnp.int32, (t, LANES), 1) < FOX_HEADS, lfn_ref[...], 0.0)
        tri = (lax.broadcasted_iota(jnp.int32, (t, t), 0) >= lax.broadcasted_iota(jnp.int32, (t, t), 1))
        l_new = _dot(tri.astype(F32), lfn, precision=HIGHEST)
        l_pad = jnp.concatenate([l_new, jnp.zeros((LANES - t, LANES), F32)], axis=0)
        l_t = _dot_nt(head_sel, l_pad, precision=HIGHEST)
        lcol = jnp.sum(jnp.where(lane == row % t, l_t, 0.0), axis=-1, keepdims=True)
        lcol_ref[...] = lcol
        zpad = jnp.zeros((LANES - t, D_MODEL), F32)
        k_pad = jnp.concatenate([kn_ref[...], zpad], axis=0).astype(BF16)
        v_pad = jnp.concatenate([vn_ref[...], zpad], axis=0).astype(BF16)
        s = _dot_nt(qbd_ref[...], k_pad) * scale + lcol - l_t
        s = jnp.where(lane <= row % t, s, NEG_BIG)
        online_update(s, v_pad)

    @pl.when(j > 0)
    def _():
        after = (lax.broadcasted_iota(jnp.int32, (PAGE_SIZE, PAGE_SIZE), 0)
                 > lax.broadcasted_iota(jnp.int32, (PAGE_SIZE, PAGE_SIZE), 1)).astype(F32)
        carry = carry_ref[...]
        later = []
        for g in range(g_pages):
            ps = slice(g * PAGE_SIZE, (g + 1) * PAGE_SIZE)
            for h in range(FOX_HEADS):
                hs = slice(h * FOX_HD, (h + 1) * FOX_HD)
                kbuf_ref[ps, hs] = k_pages[g][:, h, :].astype(BF16)
                vbuf_ref[ps, hs] = v_pages[g][:, h, :].astype(BF16)
            lfw_ref[g * PAGE_SIZE:g * PAGE_SIZE + FOX_HEADS, :] = lf_pages[g][...]
            lf = lfw_ref[ps, :]
            later.append(_dot(head_sel, _dot(lf, after, precision=HIGHEST) + carry, precision=HIGHEST))
            carry = carry + jnp.sum(lf, axis=-1, keepdims=True)
        carry_ref[...] = carry
        g_t = jnp.concatenate(later, axis=1)
        s = _dot_nt(qbd_ref[...], kbuf_ref[...]) * scale + lcol_ref[...] + g_t
        online_update(s, vbuf_ref[...])

    @pl.when(j == pl.num_programs(1) - 1)
    def _():
        out = acc_ref[...] / l_ref[...]
        for h in range(FOX_HEADS):
            o_ref[:, h * FOX_HD:(h + 1) * FOX_HD] = out[h * t:(h + 1) * t, h * FOX_HD:(h + 1) * FOX_HD]


def _fox_sample(qs, ks, vs, lfs, cache_k, cache_v, cache_lf_t, page_table, n_batch, t):
    n_pages = page_table.shape[1]
    g_pages = PAGES_PER_STEP
    steps = 1 + n_pages // g_pages
    pt_flat = page_table.reshape(-1)

    def page_map(g, rank):
        def index_map(b, j, pt):
            idx = n_pages - 1 - (jnp.maximum(j, 1) - 1) * g_pages - g
            return (0, pt[b * n_pages + idx]) + (0,) * (rank - 2)
        return index_map

    row_spec = lambda w: pl.BlockSpec((t, w), lambda b, j, pt: (b, 0))
    in_specs = [row_spec(D_MODEL), row_spec(D_MODEL), row_spec(D_MODEL), row_spec(LANES)]
    kv_block = (None, None, PAGE_SIZE, FOX_HEADS, FOX_HD)
    in_specs += [pl.BlockSpec(kv_block, page_map(g, 5)) for g in range(g_pages)]
    in_specs += [pl.BlockSpec(kv_block, page_map(g, 5)) for g in range(g_pages)]
    in_specs += [pl.BlockSpec((None, None, FOX_HEADS, PAGE_SIZE), page_map(g, 4)) for g in range(g_pages)]
    rows = FOX_HEADS * t
    grid_spec = pltpu.PrefetchScalarGridSpec(
        num_scalar_prefetch=1,
        grid=(n_batch, steps),
        in_specs=in_specs,
        out_specs=pl.BlockSpec((t, D_MODEL), lambda b, j, pt: (b, 0)),
        scratch_shapes=[pltpu.VMEM((rows, D_MODEL), BF16), pltpu.VMEM((rows, D_MODEL), F32),
                        pltpu.VMEM((rows, 1), F32), pltpu.VMEM((rows, 1), F32), pltpu.VMEM((rows, 1), F32),
                        pltpu.VMEM((rows, 1), F32),
                        pltpu.VMEM((g_pages * PAGE_SIZE, D_MODEL), BF16),
                        pltpu.VMEM((g_pages * PAGE_SIZE, D_MODEL), BF16),
                        pltpu.VMEM((g_pages * PAGE_SIZE, LANES), F32)],
    )
    return pl.pallas_call(
        _fox_sample_kernel,
        grid_spec=grid_spec,
        out_shape=jax.ShapeDtypeStruct((n_batch * t, D_MODEL), F32),
        compiler_params=_params(("arbitrary", "arbitrary")),
        name="fox_sample",
    )(pt_flat, qs, ks, vs, lfs, *([cache_k] * g_pages), *([cache_v] * g_pages), *([cache_lf_t] * g_pages))


def _pad_cols(w, width):
    return jnp.pad(w, ((0, 0), (0, width - w.shape[1])))


def kernel(x_prompt, x_sample, state_pool, state_gla, cache_k, cache_v, cache_logf, page_table, meta_tokens,
           norm_mix_e, w_in_e, pool_w, pool_scale, gla_w_a2, gla_b_a, gla_norm, w_out_e, norm_mix_o, w_in_o,
           fox_f_bias, w_out_o, norm_mlp, w_up, w_down, norm_final):
    n_bp, seq_p = x_prompt.shape[0], x_prompt.shape[1] + N_META
    n_bs, seq_s = x_sample.shape[0], x_sample.shape[1]
    meta = jnp.broadcast_to(meta_tokens[None].astype(x_prompt.dtype), (n_bp, N_META, D_MODEL))
    hp = jnp.concatenate([meta, x_prompt], axis=1).reshape(n_bp * seq_p, D_MODEL)
    hs = x_sample.reshape(n_bs * seq_s, D_MODEL)
    row = lambda v: v.reshape(1, -1)

    w_e = w_in_e[0].astype(BF16)
    n_main = D_POOL + 2 * HK + 2 * HV
    (zp, zxp), (zs, zxs) = _in_proj(hp, hs, row(norm_mix_e[0]), w_e[:, :n_main], _pad_cols(w_e[:, n_main:], LANES),
                                    jnp.zeros((1, LANES), F32), sections=(n_main // 1024,), tn=1024,
                                    logsig_extra=False)
    pw = pool_w[0].astype(BF16)
    ps = row(pool_scale[0])
    wa2 = jnp.pad(gla_w_a2[0], ((0, LANES - GLA_RANK), (0, 0))).astype(BF16)
    ba = row(gla_b_a[0])
    gn = row(gla_norm[0])
    pool_p, pool_buf_p = _pool_prompt(zp, pw, ps, n_bp)
    gla_p, gla_state_p = _gla_prompt(zp.reshape(n_bp, seq_p, n_main), zxp.reshape(n_bp, seq_p, LANES), wa2, ba, gn)
    pool_s, gla_s, pool_buf_s, gla_state_s = _even_sample(zs, zxs, state_pool, state_gla, pw, ps, wa2, ba, gn,
                                                          n_bs, seq_s)
    hp, hs = _out_proj(hp, hs, w_out_e[0].astype(BF16), [pool_p, gla_p.reshape(n_bp * seq_p, HV)], [pool_s, gla_s])
    hp, hs = _mlp(hp, hs, row(norm_mlp[0]), row(norm_final), w_up[0].astype(BF16), w_down[0].astype(BF16),
                  final_norm=False)

    w_o = w_in_o[0].astype(BF16)
    fb = jnp.pad(row(fox_f_bias[0]), ((0, 0), (0, LANES - FOX_HEADS)))
    (qp, kp, vp, lfp), (qs, ks, vs, lfs) = _in_proj(
        hp, hs, row(norm_mix_o[0]), w_o[:, :3 * D_MODEL], _pad_cols(w_o[:, 3 * D_MODEL:], LANES), fb,
        sections=(4, 4, 4), tn=512, logsig_extra=True)
    shape3 = lambda a: a.reshape(n_bp, seq_p, a.shape[-1])
    f3, ft4 = _forget_cumsum(shape3(lfp))
    att_p = _fox_prompt(shape3(qp), shape3(kp), shape3(vp), f3, ft4)
    att_s = _fox_sample(qs, ks, vs, lfs, cache_k, cache_v, jnp.swapaxes(cache_logf, 2, 3), page_table, n_bs, seq_s)
    hp, hs = _out_proj(hp, hs, w_out_o[0].astype(BF16), [att_p.reshape(n_bp * seq_p, D_MODEL)], [att_s])
    yp, ys = _mlp(hp, hs, row(norm_mlp[1]), row(norm_final), w_up[1].astype(BF16), w_down[1].astype(BF16),
                  final_norm=True)

    y_prompt = yp.reshape(n_bp, seq_p, D_MODEL)[:, N_META:]
    y_sample = ys.reshape(n_bs, seq_s, D_MODEL)
    heads = lambda a, b, s: a.reshape(1, b, s, FOX_HEADS, FOX_HD)
    lf_out = lambda a, b, s: a[:, :FOX_HEADS].reshape(1, b, s, FOX_HEADS)
    return (y_prompt, y_sample, pool_buf_p, pool_buf_s, gla_state_p, gla_state_s,
            heads(kp, n_bp, seq_p), heads(ks, n_bs, seq_s), heads(vp, n_bp, seq_p), heads(vs, n_bs, seq_s),
            lf_out(lfp, n_bp, seq_p), lf_out(lfs, n_bs, seq_s))
```

```python
import functools

import jax
import jax.numpy as jnp
from jax import lax
from jax.experimental import pallas as pl
from jax.experimental.pallas import tpu as pltpu

F32 = jnp.float32
BF16 = jnp.bfloat16
HIGHEST = lax.Precision.HIGHEST

D_MODEL = 2048
N_META = 16
EPS = 1e-6
POOL_WINDOWS = (2, 4, 8, 16)
D_POOL = 1024
POOL_GROUP = 256
POOL_BUF = 15
GLA_HEADS = 4
GLA_DK = 128
GLA_DV = 256
GLA_RANK = 16
GLA_NORMALIZER = 16.0
FOX_HEADS = 16
FOX_HD = 128
PAGE_SIZE = 128
D_FF = 8192
HK = GLA_HEADS * GLA_DK
HV = GLA_HEADS * GLA_DV

LANES = 128
NEG_BIG = -1e30
VMEM_LIMIT = 56 * 1024 * 1024

ROW_BLOCK = 1032
SEQ_BLOCK = 688
GLA_CHUNK = 48
PAGES_PER_STEP = 4

NT_DIMS = (((1,), (1,)), ((), ()))
TN_DIMS = (((0,), (0,)), ((), ()))


def _params(semantics):
    return pltpu.CompilerParams(dimension_semantics=semantics, vmem_limit_bytes=VMEM_LIMIT)


def _dot(a, b, **kw):
    return jnp.dot(a, b, preferred_element_type=F32, **kw)


def _dot_nt(a, b, **kw):
    return lax.dot_general(a, b, NT_DIMS, preferred_element_type=F32, **kw)


def _dot_tn(a, b, **kw):
    return lax.dot_general(a, b, TN_DIMS, preferred_element_type=F32, **kw)


def _log_sigmoid(x):
    return jnp.minimum(x, 0.0) - jnp.log1p(jnp.exp(-jnp.abs(x)))


def _rms_norm(x, g):
    ms = jnp.mean(x * x, axis=-1, keepdims=True)
    return x * lax.rsqrt(ms + EPS) * g


def _in_proj_kernel(sections, logsig_extra, xp_ref, xs_ref, g_ref, w_ref, wx_ref, bx_ref, *refs):
    nsec = len(sections)
    outs_p, zxp_ref = refs[:nsec], refs[nsec]
    outs_s, zxs_ref = refs[nsec + 1:2 * nsec + 1], refs[2 * nsec + 1]
    xnp_ref, xns_ref = refs[2 * nsec + 2:]
    i = pl.program_id(0)
    j = pl.program_id(1)

    def extra(xn):
        e = _dot(xn, wx_ref[...]) + bx_ref[...]
        return _log_sigmoid(e) if logsig_extra else e

    @pl.when(j == 0)
    def _():
        xn = _rms_norm(xp_ref[...], g_ref[...]).astype(BF16)
        xnp_ref[...] = xn
        zxp_ref[...] = extra(xn)

    @pl.when((j == 0) & (i == 0))
    def _():
        xn = _rms_norm(xs_ref[...], g_ref[...]).astype(BF16)
        xns_ref[...] = xn
        zxs_ref[...] = extra(xn)

    def write(outs, x_ref):
        z = _dot(x_ref[...], w_ref[...])
        off = 0
        for o_ref, n in zip(outs, sections):
            if nsec == 1:
                o_ref[...] = z.astype(o_ref.dtype)
            else:
                @pl.when((j >= off) & (j < off + n))
                def _(o_ref=o_ref):
                    o_ref[...] = z.astype(o_ref.dtype)
            off += n

    write(outs_p, xnp_ref)

    @pl.when(i == 0)
    def _():
        write(outs_s, xns_ref)


def _in_proj(xp, xs, g, w, wx, bx, sections, tn, logsig_extra):
    n_p, n_s = xp.shape[0], xs.shape[0]
    ncol = sum(sections)
    grid = (n_p // ROW_BLOCK, ncol)
    offs = [sum(sections[:k]) for k in range(len(sections))]

    def p_map(off, n):
        return lambda i, j: (i, jnp.clip(j - off, 0, n - 1))

    def s_map(off, n):
        return lambda i, j: (0, jnp.clip(jnp.where(i == 0, j, ncol - 1) - off, 0, n - 1))

    out_shape, out_specs = [], []
    for rows, tm, mk in ((n_p, ROW_BLOCK, p_map), (n_s, n_s, s_map)):
        for off, n in zip(offs, sections):
            out_shape.append(jax.ShapeDtypeStruct((rows, n * tn), F32))
            out_specs.append(pl.BlockSpec((tm, tn), mk(off, n)))
        out_shape.append(jax.ShapeDtypeStruct((rows, LANES), F32))
        out_specs.append(pl.BlockSpec((tm, LANES), (lambda i, j: (i, 0)) if rows == n_p else (lambda i, j: (0, 0))))
    outs = pl.pallas_call(
        functools.partial(_in_proj_kernel, tuple(sections), logsig_extra),
        grid=grid,
        in_specs=[
            pl.BlockSpec((ROW_BLOCK, D_MODEL), lambda i, j: (i, 0)),
            pl.BlockSpec((n_s, D_MODEL), lambda i, j: (0, 0)),
            pl.BlockSpec((1, D_MODEL), lambda i, j: (0, 0)),
            pl.BlockSpec((None, D_MODEL, tn), lambda i, j: (0, 0, j)),
            pl.BlockSpec((D_MODEL, LANES), lambda i, j: (0, 0)),
            pl.BlockSpec((1, LANES), lambda i, j: (0, 0)),
        ],
        out_specs=out_specs,
        out_shape=out_shape,
        scratch_shapes=[pltpu.VMEM((ROW_BLOCK, D_MODEL), BF16), pltpu.VMEM((n_s, D_MODEL), BF16)],
        compiler_params=_params(("arbitrary", "arbitrary")),
        name="in_proj",
    )(xp, xs, g, w, wx, bx)
    k = len(sections) + 1
    return outs[:k], outs[k:]


def _out_proj_kernel(widths, xp_ref, xs_ref, w_ref, *refs):
    na = len(widths)
    ap, a_s = refs[:na], refs[na:2 * na]
    op_ref, os_ref = refs[2 * na:]
    i = pl.program_id(0)

    def compute(x_ref, a_refs, o_ref):
        acc = x_ref[...]
        off = 0
        for a_ref, kw in zip(a_refs, widths):
            acc = acc + _dot(a_ref[...].astype(BF16), w_ref[off:off + kw, :])
            off += kw
        o_ref[...] = acc

    compute(xp_ref, ap, op_ref)

    @pl.when(i == 0)
    def _():
        compute(xs_ref, a_s, os_ref)


def _out_proj(xp, xs, w, a_p, a_s, tn=1024):
    n_p, n_s = xp.shape[0], xs.shape[0]
    widths = tuple(a.shape[1] for a in a_p)
    ncol = D_MODEL // tn
    grid = (n_p // ROW_BLOCK, ncol)
    s_col = lambda i, j: (0, jnp.where(i == 0, j, ncol - 1))
    in_specs = [
        pl.BlockSpec((ROW_BLOCK, tn), lambda i, j: (i, j)),
        pl.BlockSpec((n_s, tn), s_col),
        pl.BlockSpec((None, D_MODEL, tn), lambda i, j: (0, 0, j)),
    ]
    in_specs += [pl.BlockSpec((ROW_BLOCK, kw), lambda i, j: (i, 0)) for kw in widths]
    in_specs += [pl.BlockSpec((n_s, kw), lambda i, j: (0, 0)) for kw in widths]
    return pl.pallas_call(
        functools.partial(_out_proj_kernel, widths),
        grid=grid,
        in_specs=in_specs,
        out_specs=[pl.BlockSpec((ROW_BLOCK, tn), lambda i, j: (i, j)), pl.BlockSpec((n_s, tn), s_col)],
        out_shape=[jax.ShapeDtypeStruct((n_p, D_MODEL), F32), jax.ShapeDtypeStruct((n_s, D_MODEL), F32)],
        compiler_params=_params(("arbitrary", "arbitrary")),
        name="out_proj",
    )(xp, xs, w, *a_p, *a_s)


def _mlp_kernel(final_norm, xp_ref, xs_ref, g_ref, gf_ref, wu_ref, wd_ref, op_ref, os_ref, xnp_ref, xns_ref):
    i = pl.program_id(0)
    j = pl.program_id(1)
    last = pl.num_programs(1) - 1

    def step(x_ref, xn_ref, o_ref):
        @pl.when(j == 0)
        def _():
            x = x_ref[...]
            xn_ref[...] = _rms_norm(x, g_ref[...]).astype(BF16)
            o_ref[...] = x

        a = _dot(xn_ref[...], wu_ref[...])
        a = jnp.square(jnp.maximum(a, 0.0)).astype(BF16)
        o_ref[...] += _dot(a, wd_ref[...])
        if final_norm:
            @pl.when(j == last)
            def _():
                o_ref[...] = _rms_norm(o_ref[...], gf_ref[...])

    step(xp_ref, xnp_ref, op_ref)

    @pl.when(i == 0)
    def _():
        step(xs_ref, xns_ref, os_ref)


def _mlp(xp, xs, g, gf, wu, wd, layer, final_norm, tf=512):
    n_p, n_s = xp.shape[0], xs.shape[0]
    grid = (n_p // ROW_BLOCK, D_FF // tf)
    return pl.pallas_call(
        functools.partial(_mlp_kernel, final_norm),
        grid=grid,
        in_specs=[
            pl.BlockSpec((ROW_BLOCK, D_MODEL), lambda i, j: (i, 0)),
            pl.BlockSpec((n_s, D_MODEL), lambda i, j: (0, 0)),
            pl.BlockSpec((1, D_MODEL), lambda i, j: (0, 0)),
            pl.BlockSpec((1, D_MODEL), lambda i, j: (0, 0)),
            pl.BlockSpec((None, D_MODEL, tf), lambda i, j: (layer, 0, j)),
            pl.BlockSpec((None, tf, D_MODEL), lambda i, j: (layer, j, 0)),
        ],
        out_specs=[pl.BlockSpec((ROW_BLOCK, D_MODEL), lambda i, j: (i, 0)),
                   pl.BlockSpec((n_s, D_MODEL), lambda i, j: (0, 0))],
        out_shape=[jax.ShapeDtypeStruct((n_p, D_MODEL), F32), jax.ShapeDtypeStruct((n_s, D_MODEL), F32)],
        scratch_shapes=[pltpu.VMEM((ROW_BLOCK, D_MODEL), BF16), pltpu.VMEM((n_s, D_MODEL), BF16)],
        compiler_params=_params(("arbitrary", "arbitrary")),
        name="mlp",
    )(xp, xs, g, gf, wu, wd)


def _pool_groups(ext_ref, rows, count_fn, pw_ref, ps_ref, o_ref):
    for g, w in enumerate(POOL_WINDOWS):
        cs = slice(g * POOL_GROUP, (g + 1) * POOL_GROUP)
        u = ext_ref[16:16 + rows, cs]
        acc = u
        for back in range(1, w):
            acc = acc + ext_ref[16 - back:16 - back + rows, cs]
        pooled = acc / count_fn(w) - u
        mixed = _dot(pooled.astype(BF16), pw_ref[g]) * ps_ref[:, cs]
        o_ref[:, cs] = mixed.astype(o_ref.dtype)


def _pool_prompt_kernel(z_ref, pw_ref, ps_ref, o_ref, buf_ref, ext_ref):
    blk = pl.program_id(0) % (2064 // SEQ_BLOCK)

    @pl.when(blk == 0)
    def _():
        ext_ref[0:16, :] = jnp.zeros((16, D_POOL), F32)

    ext_ref[16:16 + SEQ_BLOCK, :] = z_ref[...]
    pos = blk * SEQ_BLOCK + lax.broadcasted_iota(jnp.int32, (SEQ_BLOCK, 1), 0)
    _pool_groups(ext_ref, SEQ_BLOCK, lambda w: jnp.minimum(w, pos + 1).astype(F32), pw_ref, ps_ref, o_ref)

    @pl.when(blk == 2064 // SEQ_BLOCK - 1)
    def _():
        buf_ref[...] = ext_ref[16 + SEQ_BLOCK - POOL_BUF:16 + SEQ_BLOCK, :]

    ext_ref[0:16, :] = ext_ref[SEQ_BLOCK:SEQ_BLOCK + 16, :]


def _pool_prompt(z_main, pw, ps, n_batch):
    n_p = z_main.shape[0]
    per_seq = 2064 // SEQ_BLOCK
    return pl.pallas_call(
        _pool_prompt_kernel,
        grid=(n_p // SEQ_BLOCK,),
        in_specs=[
            pl.BlockSpec((SEQ_BLOCK, D_POOL), lambda r: (r, 0)),
            pl.BlockSpec((len(POOL_WINDOWS), POOL_GROUP, POOL_GROUP), lambda r: (0, 0, 0)),
            pl.BlockSpec((1, D_POOL), lambda r: (0, 0)),
        ],
        out_specs=[pl.BlockSpec((SEQ_BLOCK, D_POOL), lambda r: (r, 0)),
                   pl.BlockSpec((None, None, POOL_BUF, D_POOL), lambda r: (0, r // per_seq, 0, 0))],
        out_shape=[jax.ShapeDtypeStruct((n_p, D_POOL), BF16),
                   jax.ShapeDtypeStruct((1, n_batch, POOL_BUF, D_POOL), F32)],
        scratch_shapes=[pltpu.VMEM((16 + SEQ_BLOCK, D_POOL), F32)],
        compiler_params=_params(("arbitrary",)),
        name="pool_prompt",
    )(z_main, pw, ps)


def _gla_chunk(qk, v, r, alr, wa2, ba, gnorm, st_get, st_set, o_set):
    c = qk.shape[0]
    x = _dot(alr.astype(BF16), wa2) + ba
    la = _log_sigmoid(x) * (1.0 / GLA_NORMALIZER)
    ri = lax.broadcasted_iota(jnp.int32, (c, c), 0)
    ci = lax.broadcasted_iota(jnp.int32, (c, c), 1)
    causal = ri >= ci
    b = _dot(causal.astype(F32), la, precision=HIGHEST)
    mid = c // 2 - 1
    bm = b[mid:mid + 1, :]
    be = b[c - 1:c, :]
    q = qk[:, :HK] * (GLA_DK ** -0.5)
    k = qk[:, HK:]
    qt = q * jnp.exp(b - bm)
    kt = k * jnp.exp(bm - b)
    qi = qt * jnp.exp(bm)
    khat = kt * jnp.exp(be - bm)
    e_end = jnp.exp(be)
    for h in range(GLA_HEADS):
        ks = slice(h * GLA_DK, (h + 1) * GLA_DK)
        vs = slice(h * GLA_DV, (h + 1) * GLA_DV)
        vh = v[:, vs].astype(BF16)
        att = _dot_nt(qt[:, ks].astype(BF16), kt[:, ks].astype(BF16))
        att = jnp.where(causal, att, 0.0)
        st = st_get(h)
        o = _dot(att.astype(BF16), vh) + _dot_nt(qi[:, ks].astype(BF16), st.astype(BF16))
        st_set(h, st * e_end[:, ks] + _dot_tn(vh, khat[:, ks].astype(BF16)))
        on = _rms_norm(o, gnorm[:, vs])
        rh = r[:, vs]
        gate = rh / (1.0 + jnp.exp(-rh))
        o_set(h, on * gate)


def _gla_prompt_kernel(n_batch, qk_ref, v_ref, r_ref, alr_ref, wa2_ref, ba_ref, gn_ref, o_ref, s_ref, st_ref):
    c = pl.program_id(0)

    @pl.when(c == 0)
    def _():
        st_ref[...] = jnp.zeros(st_ref.shape, F32)

    for bi in range(n_batch):
        def st_get(h, bi=bi):
            return st_ref[bi * GLA_HEADS + h]

        def st_set(h, val, bi=bi):
            st_ref[bi * GLA_HEADS + h] = val

        def o_set(h, val, bi=bi):
            o_ref[bi, :, h * GLA_DV:(h + 1) * GLA_DV] = val.astype(o_ref.dtype)

        _gla_chunk(qk_ref[bi], v_ref[bi], r_ref[bi], alr_ref[bi], wa2_ref[...], ba_ref[...], gn_ref[...],
                   st_get, st_set, o_set)

    @pl.when(c == pl.num_programs(0) - 1)
    def _():
        for bi in range(n_batch):
            for h in range(GLA_HEADS):
                s_ref[0, bi, h] = st_ref[bi * GLA_HEADS + h].T


def _gla_prompt(z_main3, zx3, wa2, ba, gn):
    n_batch, seq = z_main3.shape[0], z_main3.shape[1]
    blk = lambda col: pl.BlockSpec((n_batch, GLA_CHUNK, 1024), lambda c, col=col: (0, c, col))
    const2 = lambda shape: pl.BlockSpec(shape, lambda c: (0, 0))
    return pl.pallas_call(
        functools.partial(_gla_prompt_kernel, n_batch),
        grid=(seq // GLA_CHUNK,),
        in_specs=[blk(1), blk(2), blk(3),
                  pl.BlockSpec((n_batch, GLA_CHUNK, LANES), lambda c: (0, c, 0)),
                  const2((LANES, HK)), const2((1, HK)), const2((1, HV))],
        out_specs=[pl.BlockSpec((n_batch, GLA_CHUNK, HV), lambda c: (0, c, 0)),
                   pl.BlockSpec((1, n_batch, GLA_HEADS, GLA_DK, GLA_DV), lambda c: (0, 0, 0, 0, 0))],
        out_shape=[jax.ShapeDtypeStruct((n_batch, seq, HV), BF16),
                   jax.ShapeDtypeStruct((1, n_batch, GLA_HEADS, GLA_DK, GLA_DV), F32)],
        scratch_shapes=[pltpu.VMEM((n_batch * GLA_HEADS, GLA_DV, GLA_DK), F32)],
        compiler_params=_params(("arbitrary",)),
        name="gla_prompt",
    )(z_main3, z_main3, z_main3, zx3, wa2, ba, gn)


def _even_sample_kernel(u_ref, qk_ref, v_ref, r_ref, alr_ref, pbuf_ref, s0_ref, pw_ref, ps_ref, wa2_ref, ba_ref,
                        gn_ref, po_ref, go_ref, nbuf_ref, ns_ref, ext_ref):
    t = u_ref.shape[0]
    ext_ref[0:1, :] = jnp.zeros((1, D_POOL), F32)
    ext_ref[1:16, :] = pbuf_ref[...]
    ext_ref[16:16 + t, :] = u_ref[...]
    _pool_groups(ext_ref, t, lambda w: float(w), pw_ref, ps_ref, po_ref)
    nbuf_ref[...] = ext_ref[16 + t - POOL_BUF:16 + t, :]

    def st_get(h):
        return s0_ref[h].T

    def st_set(h, val):
        ns_ref[h] = val.T

    def o_set(h, val):
        go_ref[:, h * GLA_DV:(h + 1) * GLA_DV] = val

    _gla_chunk(qk_ref[...], v_ref[...], r_ref[...], alr_ref[...], wa2_ref[...], ba_ref[...], gn_ref[...],
               st_get, st_set, o_set)


def _even_sample(zs_main, zxs, pool_buf, gla_s0, pw, ps, wa2, ba, gn, n_batch, t):
    blk = lambda col: pl.BlockSpec((t, 1024), lambda b, col=col: (b, col))
    const = lambda shape: pl.BlockSpec(shape, lambda b: (0,) * len(shape))
    return pl.pallas_call(
        _even_sample_kernel,
        grid=(n_batch,),
        in_specs=[blk(0), blk(1), blk(2), blk(3),
                  pl.BlockSpec((t, LANES), lambda b: (b, 0)),
                  pl.BlockSpec((None, None, POOL_BUF, D_POOL), lambda b: (0, b, 0, 0)),
                  pl.BlockSpec((None, None, GLA_HEADS, GLA_DK, GLA_DV), lambda b: (0, b, 0, 0, 0)),
                  const((len(POOL_WINDOWS), POOL_GROUP, POOL_GROUP)), const((1, D_POOL)),
                  const((LANES, HK)), const((1, HK)), const((1, HV))],
        out_specs=[pl.BlockSpec((t, D_POOL), lambda b: (b, 0)),
                   pl.BlockSpec((t, HV), lambda b: (b, 0)),
                   pl.BlockSpec((None, None, POOL_BUF, D_POOL), lambda b: (0, b, 0, 0)),
                   pl.BlockSpec((None, None, GLA_HEADS, GLA_DK, GLA_DV), lambda b: (0, b, 0, 0, 0))],
        out_shape=[jax.ShapeDtypeStruct((n_batch * t, D_POOL), F32),
                   jax.ShapeDtypeStruct((n_batch * t, HV), F32),
                   jax.ShapeDtypeStruct((1, n_batch, POOL_BUF, D_POOL), F32),
                   jax.ShapeDtypeStruct((1, n_batch, GLA_HEADS, GLA_DK, GLA_DV), F32)],
        scratch_shapes=[pltpu.VMEM((16 + t, D_POOL), F32)],
        compiler_params=_params(("arbitrary",)),
        name="even_sample",
    )(zs_main, zs_main, zs_main, zs_main, zxs, pool_buf, gla_s0, pw, ps, wa2, ba, gn)


HEADS_PER_STEP = 4
AUG_LANES = 8


def _split3(x):
    hi = x.astype(BF16)
    r1 = x - hi.astype(F32)
    mid = r1.astype(BF16)
    lo = (r1 - mid.astype(F32)).astype(BF16)
    return hi, mid, lo


def _forget_cumsum_kernel(lf_ref, augq_ref, augk_ref, a_ref, b_ref):
    seq = lf_ref.shape[0]
    pad = a_ref.shape[0] - seq
    a_ref[0:pad, :] = jnp.zeros((pad, LANES), F32)
    b_ref[0:pad, :] = jnp.zeros((pad, LANES), F32)
    a_ref[pad:, :] = lf_ref[...]
    src, dst = a_ref, b_ref
    shift = 1
    while shift < seq:
        dst[pad:, :] = src[pad:, :] + src[pad - shift:pad - shift + seq, :]
        src, dst = dst, src
        shift *= 2
    parts = _split3(src[pad:, :] * (FOX_HD ** 0.5))
    head = lax.broadcasted_iota(jnp.int32, (LANES, LANES), 0)
    dest = lax.broadcasted_iota(jnp.int32, (LANES, LANES), 1)
    lane = lax.broadcasted_iota(jnp.int32, (seq, LANES), 1)
    used = lane < HEADS_PER_STEP * AUG_LANES
    ones_q = (used & (lane % AUG_LANES >= 3) & (lane % AUG_LANES < 6)).astype(F32)
    ones_k = (used & (lane % AUG_LANES < 3)).astype(F32)
    for hg in range(FOX_HEADS // HEADS_PER_STEP):
        from_head = (head == hg * HEADS_PER_STEP + dest // AUG_LANES) & (dest < HEADS_PER_STEP * AUG_LANES)
        ft = sum(_dot(parts[c], (from_head & (dest % AUG_LANES == c)).astype(BF16)) for c in range(3))
        fs = sum(_dot(parts[c], (from_head & (dest % AUG_LANES == 3 + c)).astype(BF16)) for c in range(3))
        augq_ref[hg] = (ft + ones_q).astype(BF16)
        augk_ref[hg] = (ones_k - fs).astype(BF16)


def _forget_cumsum(lf3):
    n_batch, seq = lf3.shape[0], lf3.shape[1]
    pad = 2048
    n_groups = FOX_HEADS // HEADS_PER_STEP
    aug_spec = pl.BlockSpec((None, n_groups, seq, LANES), lambda b: (b, 0, 0, 0))
    aug_shape = jax.ShapeDtypeStruct((n_batch, n_groups, seq, LANES), BF16)
    return pl.pallas_call(
        _forget_cumsum_kernel,
        grid=(n_batch,),
        in_specs=[pl.BlockSpec((None, seq, LANES), lambda b: (b, 0, 0))],
        out_specs=[aug_spec, aug_spec],
        out_shape=[aug_shape, aug_shape],
        scratch_shapes=[pltpu.VMEM((pad + seq, LANES), F32), pltpu.VMEM((pad + seq, LANES), F32)],
        compiler_params=_params(("arbitrary",)),
        name="forget_cumsum",
    )(lf3)


def _fox_prompt_kernel(q_ref, k_ref, v_ref, augq_ref, augk_ref, o_ref, m_ref, acc_ref, qa_ref):
    qi = pl.program_id(2)
    ki = pl.program_id(3)
    tq = q_ref.shape[0]
    to_log2 = (FOX_HD ** -0.5) * 1.4426950408889634

    @pl.when(ki == 0)
    def _():
        m_ref[...] = jnp.full(m_ref.shape, NEG_BIG, F32)
        acc_ref[...] = jnp.zeros(acc_ref.shape, F32)
        lane = lax.broadcasted_iota(jnp.int32, (tq, LANES), 1)
        augq = augq_ref[...]
        for hh in range(HEADS_PER_STEP):
            cs = slice(hh * FOX_HD, (hh + 1) * FOX_HD)
            own = jnp.where(lane // AUG_LANES == hh, augq, jnp.zeros_like(augq))
            qa_ref[hh] = jnp.concatenate([q_ref[:, cs].astype(BF16), own], axis=1)

    def block(masked):
        augk = augk_ref[...]
        ones_col = (lax.broadcasted_iota(jnp.int32, (tq, LANES), 1) == 0).astype(BF16)
        if masked:
            visible = (lax.broadcasted_iota(jnp.int32, (tq, tq), 1) <= lax.broadcasted_iota(jnp.int32, (tq, tq), 0))
        for hh in range(HEADS_PER_STEP):
            cs = slice(hh * FOX_HD, (hh + 1) * FOX_HD)
            ka = jnp.concatenate([k_ref[:, cs].astype(BF16), augk], axis=1)
            va = jnp.concatenate([v_ref[:, cs].astype(BF16), ones_col], axis=1)
            s = _dot_nt(qa_ref[hh], ka) * to_log2
            if masked:
                s = jnp.where(visible, s, NEG_BIG)
            m_old = m_ref[hh]
            m_new = jnp.maximum(m_old, jnp.max(s, axis=-1, keepdims=True))
            p = jnp.exp2(s - m_new)
            acc_ref[hh] = jnp.exp2(m_old - m_new) * acc_ref[hh] + _dot(p.astype(BF16), va)
            m_ref[hh] = m_new

    @pl.when(ki < qi)
    def _():
        block(masked=False)

    @pl.when(ki == qi)
    def _():
        block(masked=True)
        for hh in range(HEADS_PER_STEP):
            acc = acc_ref[hh]
            o_ref[:, hh * FOX_HD:(hh + 1) * FOX_HD] = (acc[:, :FOX_HD] / acc[:, FOX_HD:FOX_HD + 1]).astype(o_ref.dtype)


def _fox_prompt(q3, k3, v3, augq, augk):
    n_batch, seq = q3.shape[0], q3.shape[1]
    nblk = seq // SEQ_BLOCK
    wcol = HEADS_PER_STEP * FOX_HD
    kv_spec = pl.BlockSpec((None, SEQ_BLOCK, wcol), lambda b, hg, qi, ki: (b, jnp.minimum(ki, qi), hg))
    return pl.pallas_call(
        _fox_prompt_kernel,
        grid=(n_batch, FOX_HEADS // HEADS_PER_STEP, nblk, nblk),
        in_specs=[
            pl.BlockSpec((None, SEQ_BLOCK, wcol), lambda b, hg, qi, ki: (b, qi, hg)),
            kv_spec, kv_spec,
            pl.BlockSpec((None, None, SEQ_BLOCK, LANES), lambda b, hg, qi, ki: (b, hg, qi, 0)),
            pl.BlockSpec((None, None, SEQ_BLOCK, LANES), lambda b, hg, qi, ki: (b, hg, jnp.minimum(ki, qi), 0)),
        ],
        out_specs=pl.BlockSpec((None, SEQ_BLOCK, wcol), lambda b, hg, qi, ki: (b, qi, hg)),
        out_shape=jax.ShapeDtypeStruct((n_batch, seq, D_MODEL), BF16),
        scratch_shapes=[pltpu.VMEM((HEADS_PER_STEP, SEQ_BLOCK, 1), F32),
                        pltpu.VMEM((HEADS_PER_STEP, SEQ_BLOCK, 2 * FOX_HD), F32),
                        pltpu.VMEM((HEADS_PER_STEP, SEQ_BLOCK, 2 * FOX_HD), BF16)],
        compiler_params=_params(("arbitrary", "arbitrary", "arbitrary", "arbitrary")),
        name="fox_prompt",
    )(q3, k3, v3, augq, augk)


def _fox_sample_kernel(pt_ref, qs_ref, kn_ref, vn_ref, lfn_ref, *refs):
    g_pages = PAGES_PER_STEP
    k_pages = refs[:g_pages]
    v_pages = refs[g_pages:2 * g_pages]
    lf_pages = refs[2 * g_pages:3 * g_pages]
    o_ref = refs[3 * g_pages]
    qbd_ref, acc_ref, m_ref, l_ref, lcol_ref, carry_ref, kbuf_ref, vbuf_ref = refs[3 * g_pages + 1:]
    j = pl.program_id(1)
    t = qs_ref.shape[0]
    rows = FOX_HEADS * t
    scale = FOX_HD ** -0.5
    row = lax.broadcasted_iota(jnp.int32, (rows, LANES), 0)
    lane = lax.broadcasted_iota(jnp.int32, (rows, LANES), 1)
    head_sel = (row // t == lane).astype(F32)

    def online_update(s, v_bf16):
        m_old = m_ref[...]
        m_new = jnp.maximum(m_old, jnp.max(s, axis=-1, keepdims=True))
        alpha = jnp.exp(m_old - m_new)
        p = jnp.exp(s - m_new)
        l_ref[...] = alpha * l_ref[...] + jnp.sum(p, axis=-1, keepdims=True)
        acc_ref[...] = alpha * acc_ref[...] + _dot(p.astype(BF16), v_bf16)
        m_ref[...] = m_new

    @pl.when(j == 0)
    def _():
        m_ref[...] = jnp.full(m_ref.shape, NEG_BIG, F32)
        l_ref[...] = jnp.zeros(l_ref.shape, F32)
        acc_ref[...] = jnp.zeros(acc_ref.shape, F32)
        carry_ref[...] = jnp.zeros(carry_ref.shape, F32)
        q_rep = jnp.tile(qs_ref[...], (FOX_HEADS, 1))
        r2 = lax.broadcasted_iota(jnp.int32, (rows, D_MODEL), 0)
        c2 = lax.broadcasted_iota(jnp.int32, (rows, D_MODEL), 1)
        qbd_ref[...] = jnp.where(r2 // t == c2 // FOX_HD, q_rep, 0.0).astype(BF16)
        lfn = jnp.where(lax.broadcasted_iota(jnp.int32, (t, LANES), 1) < FOX_HEADS, lfn_ref[...], 0.0)
        tri = (lax.broadcasted_iota(jnp.int32, (t, t), 0) >= lax.broadcasted_iota(jnp.int32, (t, t), 1))
        l_new = _dot(tri.astype(F32), lfn, precision=HIGHEST)
        l_pad = jnp.concatenate([l_new, jnp.zeros((LANES - t, LANES), F32)], axis=0)
        l_t = _dot_nt(head_sel, l_pad, precision=HIGHEST)
        lcol = jnp.sum(jnp.where(lane == row % t, l_t, 0.0), axis=-1, keepdims=True)
        lcol_ref[...] = lcol
        zpad = jnp.zeros((LANES - t, D_MODEL), F32)
        k_pad = jnp.concatenate([kn_ref[...], zpad], axis=0).astype(BF16)
        v_pad = jnp.concatenate([vn_ref[...], zpad], axis=0).astype(BF16)
        s = _dot_nt(qbd_ref[...], k_pad) * scale + lcol - l_t
        s = jnp.where(lane <= row % t, s, NEG_BIG)
        online_update(s, v_pad)

    @pl.when(j > 0)
    def _():
        key = lax.broadcasted_iota(jnp.int32, (FOX_HEADS, PAGE_SIZE), 1)
        carry = carry_ref[...]
        later = []
        for g in range(g_pages):
            ps = slice(g * PAGE_SIZE, (g + 1) * PAGE_SIZE)
            for h in range(FOX_HEADS):
                hs = slice(h * FOX_HD, (h + 1) * FOX_HD)
                head_rows = pl.ds(h, PAGE_SIZE, stride=FOX_HEADS)
                kbuf_ref[ps, hs] = k_pages[g][head_rows, :].astype(BF16)
                vbuf_ref[ps, hs] = v_pages[g][head_rows, :].astype(BF16)
            lf = lf_pages[g][...]
            suffix = lf
            shift = 1
            while shift < PAGE_SIZE:
                moved = pltpu.roll(suffix, PAGE_SIZE - shift, axis=1)
                suffix = suffix + jnp.where(key + shift < PAGE_SIZE, moved, 0.0)
                shift *= 2
            later_h = suffix - lf + carry
            carry = carry + suffix[:, 0:1]
            later.append(jnp.concatenate(
                [jnp.broadcast_to(later_h[h:h + 1, :], (t, PAGE_SIZE)) for h in range(FOX_HEADS)], axis=0))
        carry_ref[...] = carry
        g_t = jnp.concatenate(later, axis=1)
        s = _dot_nt(qbd_ref[...], kbuf_ref[...]) * scale + lcol_ref[...] + g_t
        online_update(s, vbuf_ref[...])

    @pl.when(j == pl.num_programs(1) - 1)
    def _():
        out = acc_ref[...] / l_ref[...]
        for h in range(FOX_HEADS):
            o_ref[:, h * FOX_HD:(h + 1) * FOX_HD] = out[h * t:(h + 1) * t, h * FOX_HD:(h + 1) * FOX_HD]


def _fox_sample(qs, ks, vs, lfs, cache_k, cache_v, cache_logf, page_table, n_batch, t):
    n_pages = page_table.shape[1]
    n_pool = cache_k.shape[1]
    cache_k = cache_k.reshape(1, n_pool, PAGE_SIZE * FOX_HEADS, FOX_HD)
    cache_v = cache_v.reshape(1, n_pool, PAGE_SIZE * FOX_HEADS, FOX_HD)
    cache_lf_t = jnp.swapaxes(cache_logf, 2, 3)
    g_pages = PAGES_PER_STEP
    steps = 1 + n_pages // g_pages
    pt_flat = page_table.reshape(-1)

    def page_map(g):
        def index_map(b, j, pt):
            idx = n_pages - 1 - (jnp.maximum(j, 1) - 1) * g_pages - g
            return (0, pt[b * n_pages + idx], 0, 0)
        return index_map

    row_spec = lambda w: pl.BlockSpec((t, w), lambda b, j, pt: (b, 0))
    in_specs = [row_spec(D_MODEL), row_spec(D_MODEL), row_spec(D_MODEL), row_spec(LANES)]
    kv_block = (None, None, PAGE_SIZE * FOX_HEADS, FOX_HD)
    in_specs += [pl.BlockSpec(kv_block, page_map(g)) for g in range(g_pages)]
    in_specs += [pl.BlockSpec(kv_block, page_map(g)) for g in range(g_pages)]
    in_specs += [pl.BlockSpec((None, None, FOX_HEADS, PAGE_SIZE), page_map(g)) for g in range(g_pages)]
    rows = FOX_HEADS * t
    grid_spec = pltpu.PrefetchScalarGridSpec(
        num_scalar_prefetch=1,
        grid=(n_batch, steps),
        in_specs=in_specs,
        out_specs=pl.BlockSpec((t, D_MODEL), lambda b, j, pt: (b, 0)),
        scratch_shapes=[pltpu.VMEM((rows, D_MODEL), BF16), pltpu.VMEM((rows, D_MODEL), F32),
                        pltpu.VMEM((rows, 1), F32), pltpu.VMEM((rows, 1), F32), pltpu.VMEM((rows, 1), F32),
                        pltpu.VMEM((FOX_HEADS, 1), F32),
                        pltpu.VMEM((g_pages * PAGE_SIZE, D_MODEL), BF16),
                        pltpu.VMEM((g_pages * PAGE_SIZE, D_MODEL), BF16)],
    )
    return pl.pallas_call(
        _fox_sample_kernel,
        grid_spec=grid_spec,
        out_shape=jax.ShapeDtypeStruct((n_batch * t, D_MODEL), F32),
        compiler_params=_params(("arbitrary", "arbitrary")),
        name="fox_sample",
    )(pt_flat, qs, ks, vs, lfs, *([cache_k] * g_pages), *([cache_v] * g_pages), *([cache_lf_t] * g_pages))


def _pad_cols(w, width):
    return jnp.pad(w, ((0, 0), (0, width - w.shape[1])))


def kernel(x_prompt, x_sample, state_pool, state_gla, cache_k, cache_v, cache_logf, page_table, meta_tokens,
           norm_mix_e, w_in_e, pool_w, pool_scale, gla_w_a2, gla_b_a, gla_norm, w_out_e, norm_mix_o, w_in_o,
           fox_f_bias, w_out_o, norm_mlp, w_up, w_down, norm_final):
    n_bp, seq_p = x_prompt.shape[0], x_prompt.shape[1] + N_META
    n_bs, seq_s = x_sample.shape[0], x_sample.shape[1]
    meta = jnp.broadcast_to(meta_tokens[None].astype(x_prompt.dtype), (n_bp, N_META, D_MODEL))
    hp = jnp.concatenate([meta, x_prompt], axis=1).reshape(n_bp * seq_p, D_MODEL)
    hs = x_sample.reshape(n_bs * seq_s, D_MODEL)
    row = lambda v: v.reshape(1, -1)

    w_e = w_in_e.astype(BF16)
    w_up_bf, w_down_bf = w_up.astype(BF16), w_down.astype(BF16)
    n_main = D_POOL + 2 * HK + 2 * HV
    (zp, zxp), (zs, zxs) = _in_proj(hp, hs, row(norm_mix_e[0]), w_e, _pad_cols(w_e[0, :, n_main:], LANES),
                                    jnp.zeros((1, LANES), F32), sections=(n_main // 1024,), tn=1024,
                                    logsig_extra=False)
    pw = pool_w[0].astype(BF16)
    ps = row(pool_scale[0])
    wa2 = jnp.pad(gla_w_a2[0], ((0, LANES - GLA_RANK), (0, 0))).astype(BF16)
    ba = row(gla_b_a[0])
    gn = row(gla_norm[0])
    pool_p, pool_buf_p = _pool_prompt(zp, pw, ps, n_bp)
    gla_p, gla_state_p = _gla_prompt(zp.reshape(n_bp, seq_p, n_main), zxp.reshape(n_bp, seq_p, LANES), wa2, ba, gn)
    pool_s, gla_s, pool_buf_s, gla_state_s = _even_sample(zs, zxs, state_pool, state_gla, pw, ps, wa2, ba, gn,
                                                          n_bs, seq_s)
    hp, hs = _out_proj(hp, hs, w_out_e.astype(BF16), [pool_p, gla_p.reshape(n_bp * seq_p, HV)], [pool_s, gla_s])
    hp, hs = _mlp(hp, hs, row(norm_mlp[0]), row(norm_final), w_up_bf, w_down_bf, layer=0, final_norm=False)

    w_o = w_in_o.astype(BF16)
    fb = jnp.pad(row(fox_f_bias[0]), ((0, 0), (0, LANES - FOX_HEADS)))
    (qp, kp, vp, lfp), (qs, ks, vs, lfs) = _in_proj(
        hp, hs, row(norm_mix_o[0]), w_o, _pad_cols(w_o[0, :, 3 * D_MODEL:], LANES), fb,
        sections=(4, 4, 4), tn=512, logsig_extra=True)
    shape3 = lambda a: a.reshape(n_bp, seq_p, a.shape[-1])
    augq, augk = _forget_cumsum(shape3(lfp))
    att_p = _fox_prompt(shape3(qp), shape3(kp), shape3(vp), augq, augk)
    att_s = _fox_sample(qs, ks, vs, lfs, cache_k, cache_v, cache_logf, page_table, n_bs, seq_s)
    hp, hs = _out_proj(hp, hs, w_out_o.astype(BF16), [att_p.reshape(n_bp * seq_p, D_MODEL)], [att_s])
    yp, ys = _mlp(hp, hs, row(norm_mlp[1]), row(norm_final), w_up_bf, w_down_bf, layer=1, final_norm=True)

    y_prompt = yp.reshape(n_bp, seq_p, D_MODEL)[:, N_META:]
    y_sample = ys.reshape(n_bs, seq_s, D_MODEL)
    heads = lambda a, b, s: a.reshape(1, b, s, FOX_HEADS, FOX_HD)
    lf_out = lambda a, b, s: a[:, :FOX_HEADS].reshape(1, b, s, FOX_HEADS)
    return (y_prompt, y_sample, pool_buf_p, pool_buf_s, gla_state_p, gla_state_s,
            heads(kp, n_bp, seq_p), heads(ks, n_bs, seq_s), heads(vp, n_bp, seq_p), heads(vs, n_bs, seq_s),
            lf_out(lfp, n_bp, seq_p), lf_out(lfs, n_bs, seq_s))
```

```python
import functools

import jax
import jax.numpy as jnp
from jax import lax
from jax.experimental import pallas as pl
from jax.experimental.pallas import tpu as pltpu

F32 = jnp.float32
BF16 = jnp.bfloat16
HIGHEST = lax.Precision.HIGHEST

D_MODEL = 2048
N_META = 16
EPS = 1e-6
POOL_WINDOWS = (2, 4, 8, 16)
D_POOL = 1024
POOL_GROUP = 256
POOL_BUF = 15
GLA_HEADS = 4
GLA_DK = 128
GLA_DV = 256
GLA_RANK = 16
GLA_NORMALIZER = 16.0
FOX_HEADS = 16
FOX_HD = 128
PAGE_SIZE = 128
D_FF = 8192
HK = GLA_HEADS * GLA_DK
HV = GLA_HEADS * GLA_DV

LANES = 128
NEG_BIG = -1e30
VMEM_LIMIT = 56 * 1024 * 1024

ROW_BLOCK = 1032
SEQ_BLOCK = 688
GLA_CHUNK = 48
PAGES_PER_STEP = 4

NT_DIMS = (((1,), (1,)), ((), ()))
TN_DIMS = (((0,), (0,)), ((), ()))


def _params(semantics):
    return pltpu.CompilerParams(dimension_semantics=semantics, vmem_limit_bytes=VMEM_LIMIT)


def _dot(a, b, **kw):
    return jnp.dot(a, b, preferred_element_type=F32, **kw)


def _dot_nt(a, b, **kw):
    return lax.dot_general(a, b, NT_DIMS, preferred_element_type=F32, **kw)


def _dot_tn(a, b, **kw):
    return lax.dot_general(a, b, TN_DIMS, preferred_element_type=F32, **kw)


def _log_sigmoid(x):
    return jnp.minimum(x, 0.0) - jnp.log1p(jnp.exp(-jnp.abs(x)))


def _rms_norm(x, g):
    ms = jnp.mean(x * x, axis=-1, keepdims=True)
    return x * lax.rsqrt(ms + EPS) * g


def _in_proj_kernel(sections, logsig_extra, xp_ref, xs_ref, g_ref, w_ref, wx_ref, bx_ref, *refs):
    nsec = len(sections)
    outs_p, zxp_ref = refs[:nsec], refs[nsec]
    outs_s, zxs_ref = refs[nsec + 1:2 * nsec + 1], refs[2 * nsec + 1]
    xnp_ref, xns_ref = refs[2 * nsec + 2:]
    i = pl.program_id(0)
    j = pl.program_id(1)

    def extra(xn):
        e = _dot(xn, wx_ref[...]) + bx_ref[...]
        return _log_sigmoid(e) if logsig_extra else e

    @pl.when(j == 0)
    def _():
        xn = _rms_norm(xp_ref[...], g_ref[...]).astype(BF16)
        xnp_ref[...] = xn
        zxp_ref[...] = extra(xn)

    @pl.when((j == 0) & (i == 0))
    def _():
        xn = _rms_norm(xs_ref[...], g_ref[...]).astype(BF16)
        xns_ref[...] = xn
        zxs_ref[...] = extra(xn)

    def write(outs, x_ref):
        z = _dot(x_ref[...], w_ref[...])
        off = 0
        for o_ref, n in zip(outs, sections):
            if nsec == 1:
                o_ref[...] = z.astype(o_ref.dtype)
            else:
                @pl.when((j >= off) & (j < off + n))
                def _(o_ref=o_ref):
                    o_ref[...] = z.astype(o_ref.dtype)
            off += n

    write(outs_p, xnp_ref)

    @pl.when(i == 0)
    def _():
        write(outs_s, xns_ref)


def _in_proj(xp, xs, g, w, wx, bx, sections, tm, tn, logsig_extra):
    n_p, n_s = xp.shape[0], xs.shape[0]
    ncol = sum(sections)
    grid = (n_p // tm, ncol)
    offs = [sum(sections[:k]) for k in range(len(sections))]

    def p_map(off, n):
        return lambda i, j: (i, jnp.clip(j - off, 0, n - 1))

    def s_map(off, n):
        return lambda i, j: (0, jnp.clip(jnp.where(i == 0, j, ncol - 1) - off, 0, n - 1))

    out_shape, out_specs = [], []
    for rows, tr, mk in ((n_p, tm, p_map), (n_s, n_s, s_map)):
        for off, n in zip(offs, sections):
            out_shape.append(jax.ShapeDtypeStruct((rows, n * tn), F32))
            out_specs.append(pl.BlockSpec((tr, tn), mk(off, n)))
        out_shape.append(jax.ShapeDtypeStruct((rows, LANES), F32))
        out_specs.append(pl.BlockSpec((tr, LANES), (lambda i, j: (i, 0)) if rows == n_p else (lambda i, j: (0, 0))))
    outs = pl.pallas_call(
        functools.partial(_in_proj_kernel, tuple(sections), logsig_extra),
        grid=grid,
        in_specs=[
            pl.BlockSpec((tm, D_MODEL), lambda i, j: (i, 0)),
            pl.BlockSpec((n_s, D_MODEL), lambda i, j: (0, 0)),
            pl.BlockSpec((1, D_MODEL), lambda i, j: (0, 0)),
            pl.BlockSpec((None, D_MODEL, tn), lambda i, j: (0, 0, j)),
            pl.BlockSpec((D_MODEL, LANES), lambda i, j: (0, 0)),
            pl.BlockSpec((1, LANES), lambda i, j: (0, 0)),
        ],
        out_specs=out_specs,
        out_shape=out_shape,
        scratch_shapes=[pltpu.VMEM((tm, D_MODEL), BF16), pltpu.VMEM((n_s, D_MODEL), BF16)],
        compiler_params=_params(("arbitrary", "arbitrary")),
        name="in_proj",
    )(xp, xs, g, w, wx, bx)
    k = len(sections) + 1
    return outs[:k], outs[k:]


def _out_proj_kernel(widths, xp_ref, xs_ref, w_ref, *refs):
    na = len(widths)
    ap, a_s = refs[:na], refs[na:2 * na]
    op_ref, os_ref = refs[2 * na:]
    i = pl.program_id(0)

    def compute(x_ref, a_refs, o_ref):
        acc = x_ref[...]
        off = 0
        for a_ref, kw in zip(a_refs, widths):
            acc = acc + _dot(a_ref[...].astype(BF16), w_ref[off:off + kw, :])
            off += kw
        o_ref[...] = acc

    compute(xp_ref, ap, op_ref)

    @pl.when(i == 0)
    def _():
        compute(xs_ref, a_s, os_ref)


def _out_proj(xp, xs, w, a_p, a_s, tn=1024):
    n_p, n_s = xp.shape[0], xs.shape[0]
    widths = tuple(a.shape[1] for a in a_p)
    ncol = D_MODEL // tn
    grid = (n_p // ROW_BLOCK, ncol)
    s_col = lambda i, j: (0, jnp.where(i == 0, j, ncol - 1))
    in_specs = [
        pl.BlockSpec((ROW_BLOCK, tn), lambda i, j: (i, j)),
        pl.BlockSpec((n_s, tn), s_col),
        pl.BlockSpec((None, D_MODEL, tn), lambda i, j: (0, 0, j)),
    ]
    in_specs += [pl.BlockSpec((ROW_BLOCK, kw), lambda i, j: (i, 0)) for kw in widths]
    in_specs += [pl.BlockSpec((n_s, kw), lambda i, j: (0, 0)) for kw in widths]
    return pl.pallas_call(
        functools.partial(_out_proj_kernel, widths),
        grid=grid,
        in_specs=in_specs,
        out_specs=[pl.BlockSpec((ROW_BLOCK, tn), lambda i, j: (i, j)), pl.BlockSpec((n_s, tn), s_col)],
        out_shape=[jax.ShapeDtypeStruct((n_p, D_MODEL), F32), jax.ShapeDtypeStruct((n_s, D_MODEL), F32)],
        compiler_params=_params(("arbitrary", "arbitrary")),
        name="out_proj",
    )(xp, xs, w, *a_p, *a_s)


def _mlp_kernel(final_norm, xp_ref, xs_ref, g_ref, gf_ref, wu_ref, wd_ref, op_ref, os_ref, xnp_ref, xns_ref):
    i = pl.program_id(0)
    j = pl.program_id(1)
    last = pl.num_programs(1) - 1

    def step(x_ref, xn_ref, o_ref):
        @pl.when(j == 0)
        def _():
            x = x_ref[...]
            xn_ref[...] = _rms_norm(x, g_ref[...]).astype(BF16)
            o_ref[...] = x

        a = _dot(xn_ref[...], wu_ref[...])
        a = jnp.square(jnp.maximum(a, 0.0)).astype(BF16)
        o_ref[...] += _dot(a, wd_ref[...])
        if final_norm:
            @pl.when(j == last)
            def _():
                o_ref[...] = _rms_norm(o_ref[...], gf_ref[...])

    step(xp_ref, xnp_ref, op_ref)

    @pl.when(i == 0)
    def _():
        step(xs_ref, xns_ref, os_ref)


def _mlp(xp, xs, g, gf, wu, wd, layer, final_norm, tf=512):
    n_p, n_s = xp.shape[0], xs.shape[0]
    grid = (n_p // ROW_BLOCK, D_FF // tf)
    return pl.pallas_call(
        functools.partial(_mlp_kernel, final_norm),
        grid=grid,
        in_specs=[
            pl.BlockSpec((ROW_BLOCK, D_MODEL), lambda i, j: (i, 0)),
            pl.BlockSpec((n_s, D_MODEL), lambda i, j: (0, 0)),
            pl.BlockSpec((1, D_MODEL), lambda i, j: (0, 0)),
            pl.BlockSpec((1, D_MODEL), lambda i, j: (0, 0)),
            pl.BlockSpec((None, D_MODEL, tf), lambda i, j: (layer, 0, j)),
            pl.BlockSpec((None, tf, D_MODEL), lambda i, j: (layer, j, 0)),
        ],
        out_specs=[pl.BlockSpec((ROW_BLOCK, D_MODEL), lambda i, j: (i, 0)),
                   pl.BlockSpec((n_s, D_MODEL), lambda i, j: (0, 0))],
        out_shape=[jax.ShapeDtypeStruct((n_p, D_MODEL), F32), jax.ShapeDtypeStruct((n_s, D_MODEL), F32)],
        scratch_shapes=[pltpu.VMEM((ROW_BLOCK, D_MODEL), BF16), pltpu.VMEM((n_s, D_MODEL), BF16)],
        compiler_params=_params(("arbitrary", "arbitrary")),
        name="mlp",
    )(xp, xs, g, gf, wu, wd)


def _pool_groups(ext_ref, rows, count_fn, pw_ref, ps_ref, o_ref):
    for g, w in enumerate(POOL_WINDOWS):
        cs = slice(g * POOL_GROUP, (g + 1) * POOL_GROUP)
        u = ext_ref[16:16 + rows, cs]
        acc = u
        for back in range(1, w):
            acc = acc + ext_ref[16 - back:16 - back + rows, cs]
        pooled = acc / count_fn(w) - u
        mixed = _dot(pooled.astype(BF16), pw_ref[g]) * ps_ref[:, cs]
        o_ref[:, cs] = mixed.astype(o_ref.dtype)


def _pool_prompt_kernel(z_ref, pw_ref, ps_ref, o_ref, buf_ref, ext_ref):
    blk = pl.program_id(0) % (2064 // SEQ_BLOCK)

    @pl.when(blk == 0)
    def _():
        ext_ref[0:16, :] = jnp.zeros((16, D_POOL), F32)

    ext_ref[16:16 + SEQ_BLOCK, :] = z_ref[...]
    pos = blk * SEQ_BLOCK + lax.broadcasted_iota(jnp.int32, (SEQ_BLOCK, 1), 0)
    _pool_groups(ext_ref, SEQ_BLOCK, lambda w: jnp.minimum(w, pos + 1).astype(F32), pw_ref, ps_ref, o_ref)

    @pl.when(blk == 2064 // SEQ_BLOCK - 1)
    def _():
        buf_ref[...] = ext_ref[16 + SEQ_BLOCK - POOL_BUF:16 + SEQ_BLOCK, :]

    ext_ref[0:16, :] = ext_ref[SEQ_BLOCK:SEQ_BLOCK + 16, :]


def _pool_prompt(z_main, pw, ps, n_batch):
    n_p = z_main.shape[0]
    per_seq = 2064 // SEQ_BLOCK
    return pl.pallas_call(
        _pool_prompt_kernel,
        grid=(n_p // SEQ_BLOCK,),
        in_specs=[
            pl.BlockSpec((SEQ_BLOCK, D_POOL), lambda r: (r, 0)),
            pl.BlockSpec((len(POOL_WINDOWS), POOL_GROUP, POOL_GROUP), lambda r: (0, 0, 0)),
            pl.BlockSpec((1, D_POOL), lambda r: (0, 0)),
        ],
        out_specs=[pl.BlockSpec((SEQ_BLOCK, D_POOL), lambda r: (r, 0)),
                   pl.BlockSpec((None, None, POOL_BUF, D_POOL), lambda r: (0, r // per_seq, 0, 0))],
        out_shape=[jax.ShapeDtypeStruct((n_p, D_POOL), BF16),
                   jax.ShapeDtypeStruct((1, n_batch, POOL_BUF, D_POOL), F32)],
        scratch_shapes=[pltpu.VMEM((16 + SEQ_BLOCK, D_POOL), F32)],
        compiler_params=_params(("arbitrary",)),
        name="pool_prompt",
    )(z_main, pw, ps)


def _gla_chunk(qk, v, r, alr, wa2, ba, gnorm, st_get, st_set, o_set):
    c = qk.shape[0]
    x = _dot(alr.astype(BF16), wa2) + ba
    la = _log_sigmoid(x) * (1.0 / GLA_NORMALIZER)
    ri = lax.broadcasted_iota(jnp.int32, (c, c), 0)
    ci = lax.broadcasted_iota(jnp.int32, (c, c), 1)
    causal = ri >= ci
    b = _dot(causal.astype(F32), la, precision=HIGHEST)
    mid = c // 2 - 1
    bm = b[mid:mid + 1, :]
    be = b[c - 1:c, :]
    q = qk[:, :HK] * (GLA_DK ** -0.5)
    k = qk[:, HK:]
    qt = q * jnp.exp(b - bm)
    kt = k * jnp.exp(bm - b)
    qi = qt * jnp.exp(bm)
    khat = kt * jnp.exp(be - bm)
    e_end = jnp.exp(be)
    for h in range(GLA_HEADS):
        ks = slice(h * GLA_DK, (h + 1) * GLA_DK)
        vs = slice(h * GLA_DV, (h + 1) * GLA_DV)
        vh = v[:, vs].astype(BF16)
        att = _dot_nt(qt[:, ks].astype(BF16), kt[:, ks].astype(BF16))
        att = jnp.where(causal, att, 0.0)
        st = st_get(h)
        o = _dot(att.astype(BF16), vh) + _dot_nt(qi[:, ks].astype(BF16), st.astype(BF16))
        st_set(h, st * e_end[:, ks] + _dot_tn(vh, khat[:, ks].astype(BF16)))
        on = _rms_norm(o, gnorm[:, vs])
        rh = r[:, vs]
        gate = rh / (1.0 + jnp.exp(-rh))
        o_set(h, on * gate)


def _gla_prompt_kernel(n_batch, qk_ref, v_ref, r_ref, alr_ref, wa2_ref, ba_ref, gn_ref, o_ref, s_ref, st_ref):
    c = pl.program_id(0)

    @pl.when(c == 0)
    def _():
        st_ref[...] = jnp.zeros(st_ref.shape, F32)

    for bi in range(n_batch):
        def st_get(h, bi=bi):
            return st_ref[bi * GLA_HEADS + h]

        def st_set(h, val, bi=bi):
            st_ref[bi * GLA_HEADS + h] = val

        def o_set(h, val, bi=bi):
            o_ref[bi, :, h * GLA_DV:(h + 1) * GLA_DV] = val.astype(o_ref.dtype)

        _gla_chunk(qk_ref[bi], v_ref[bi], r_ref[bi], alr_ref[bi], wa2_ref[...], ba_ref[...], gn_ref[...],
                   st_get, st_set, o_set)

    @pl.when(c == pl.num_programs(0) - 1)
    def _():
        for bi in range(n_batch):
            for h in range(GLA_HEADS):
                s_ref[0, bi, h] = st_ref[bi * GLA_HEADS + h].T


def _gla_prompt(z_main3, zx3, wa2, ba, gn):
    n_batch, seq = z_main3.shape[0], z_main3.shape[1]
    blk = lambda col: pl.BlockSpec((n_batch, GLA_CHUNK, 1024), lambda c, col=col: (0, c, col))
    const2 = lambda shape: pl.BlockSpec(shape, lambda c: (0, 0))
    return pl.pallas_call(
        functools.partial(_gla_prompt_kernel, n_batch),
        grid=(seq // GLA_CHUNK,),
        in_specs=[blk(1), blk(2), blk(3),
                  pl.BlockSpec((n_batch, GLA_CHUNK, LANES), lambda c: (0, c, 0)),
                  const2((LANES, HK)), const2((1, HK)), const2((1, HV))],
        out_specs=[pl.BlockSpec((n_batch, GLA_CHUNK, HV), lambda c: (0, c, 0)),
                   pl.BlockSpec((1, n_batch, GLA_HEADS, GLA_DK, GLA_DV), lambda c: (0, 0, 0, 0, 0))],
        out_shape=[jax.ShapeDtypeStruct((n_batch, seq, HV), BF16),
                   jax.ShapeDtypeStruct((1, n_batch, GLA_HEADS, GLA_DK, GLA_DV), F32)],
        scratch_shapes=[pltpu.VMEM((n_batch * GLA_HEADS, GLA_DV, GLA_DK), F32)],
        compiler_params=_params(("arbitrary",)),
        name="gla_prompt",
    )(z_main3, z_main3, z_main3, zx3, wa2, ba, gn)


def _even_sample_kernel(u_ref, qk_ref, v_ref, r_ref, alr_ref, pbuf_ref, s0_ref, pw_ref, ps_ref, wa2_ref, ba_ref,
                        gn_ref, po_ref, go_ref, nbuf_ref, ns_ref, ext_ref):
    t = u_ref.shape[0]
    ext_ref[0:1, :] = jnp.zeros((1, D_POOL), F32)
    ext_ref[1:16, :] = pbuf_ref[...]
    ext_ref[16:16 + t, :] = u_ref[...]
    _pool_groups(ext_ref, t, lambda w: float(w), pw_ref, ps_ref, po_ref)
    nbuf_ref[...] = ext_ref[16 + t - POOL_BUF:16 + t, :]

    def st_get(h):
        return s0_ref[h].T

    def st_set(h, val):
        ns_ref[h] = val.T

    def o_set(h, val):
        go_ref[:, h * GLA_DV:(h + 1) * GLA_DV] = val

    _gla_chunk(qk_ref[...], v_ref[...], r_ref[...], alr_ref[...], wa2_ref[...], ba_ref[...], gn_ref[...],
               st_get, st_set, o_set)


def _even_sample(zs_main, zxs, pool_buf, gla_s0, pw, ps, wa2, ba, gn, n_batch, t):
    blk = lambda col: pl.BlockSpec((t, 1024), lambda b, col=col: (b, col))
    const = lambda shape: pl.BlockSpec(shape, lambda b: (0,) * len(shape))
    return pl.pallas_call(
        _even_sample_kernel,
        grid=(n_batch,),
        in_specs=[blk(0), blk(1), blk(2), blk(3),
                  pl.BlockSpec((t, LANES), lambda b: (b, 0)),
                  pl.BlockSpec((None, None, POOL_BUF, D_POOL), lambda b: (0, b, 0, 0)),
                  pl.BlockSpec((None, None, GLA_HEADS, GLA_DK, GLA_DV), lambda b: (0, b, 0, 0, 0)),
                  const((len(POOL_WINDOWS), POOL_GROUP, POOL_GROUP)), const((1, D_POOL)),
                  const((LANES, HK)), const((1, HK)), const((1, HV))],
        out_specs=[pl.BlockSpec((t, D_POOL), lambda b: (b, 0)),
                   pl.BlockSpec((t, HV), lambda b: (b, 0)),
                   pl.BlockSpec((None, None, POOL_BUF, D_POOL), lambda b: (0, b, 0, 0)),
                   pl.BlockSpec((None, None, GLA_HEADS, GLA_DK, GLA_DV), lambda b: (0, b, 0, 0, 0))],
        out_shape=[jax.ShapeDtypeStruct((n_batch * t, D_POOL), F32),
                   jax.ShapeDtypeStruct((n_batch * t, HV), F32),
                   jax.ShapeDtypeStruct((1, n_batch, POOL_BUF, D_POOL), F32),
                   jax.ShapeDtypeStruct((1, n_batch, GLA_HEADS, GLA_DK, GLA_DV), F32)],
        scratch_shapes=[pltpu.VMEM((16 + t, D_POOL), F32)],
        compiler_params=_params(("arbitrary",)),
        name="even_sample",
    )(zs_main, zs_main, zs_main, zs_main, zxs, pool_buf, gla_s0, pw, ps, wa2, ba, gn)


HEADS_PER_STEP = 4
AUG_LANES = 8


def _split3(x):
    hi = x.astype(BF16)
    r1 = x - hi.astype(F32)
    mid = r1.astype(BF16)
    lo = (r1 - mid.astype(F32)).astype(BF16)
    return hi, mid, lo


def _forget_cumsum_kernel(lf_ref, augq_ref, augk_ref, a_ref, b_ref):
    seq = lf_ref.shape[0]
    pad = a_ref.shape[0] - seq
    a_ref[0:pad, :] = jnp.zeros((pad, LANES), F32)
    b_ref[0:pad, :] = jnp.zeros((pad, LANES), F32)
    a_ref[pad:, :] = lf_ref[...]
    src, dst = a_ref, b_ref
    shift = 1
    while shift < seq:
        dst[pad:, :] = src[pad:, :] + src[pad - shift:pad - shift + seq, :]
        src, dst = dst, src
        shift *= 2
    parts = _split3(src[pad:, :] * (FOX_HD ** 0.5))
    head = lax.broadcasted_iota(jnp.int32, (LANES, LANES), 0)
    dest = lax.broadcasted_iota(jnp.int32, (LANES, LANES), 1)
    lane = lax.broadcasted_iota(jnp.int32, (seq, LANES), 1)
    used = lane < HEADS_PER_STEP * AUG_LANES
    ones_q = (used & (lane % AUG_LANES >= 3) & (lane % AUG_LANES < 6)).astype(F32)
    ones_k = (used & (lane % AUG_LANES < 3)).astype(F32)
    for hg in range(FOX_HEADS // HEADS_PER_STEP):
        from_head = (head == hg * HEADS_PER_STEP + dest // AUG_LANES) & (dest < HEADS_PER_STEP * AUG_LANES)
        ft = sum(_dot(parts[c], (from_head & (dest % AUG_LANES == c)).astype(BF16)) for c in range(3))
        fs = sum(_dot(parts[c], (from_head & (dest % AUG_LANES == 3 + c)).astype(BF16)) for c in range(3))
        augq_ref[hg] = (ft + ones_q).astype(BF16)
        augk_ref[hg] = (ones_k - fs).astype(BF16)


def _forget_cumsum(lf3):
    n_batch, seq = lf3.shape[0], lf3.shape[1]
    pad = 2048
    n_groups = FOX_HEADS // HEADS_PER_STEP
    aug_spec = pl.BlockSpec((None, n_groups, seq, LANES), lambda b: (b, 0, 0, 0))
    aug_shape = jax.ShapeDtypeStruct((n_batch, n_groups, seq, LANES), BF16)
    return pl.pallas_call(
        _forget_cumsum_kernel,
        grid=(n_batch,),
        in_specs=[pl.BlockSpec((None, seq, LANES), lambda b: (b, 0, 0))],
        out_specs=[aug_spec, aug_spec],
        out_shape=[aug_shape, aug_shape],
        scratch_shapes=[pltpu.VMEM((pad + seq, LANES), F32), pltpu.VMEM((pad + seq, LANES), F32)],
        compiler_params=_params(("arbitrary",)),
        name="forget_cumsum",
    )(lf3)


def _fox_prompt_kernel(q_ref, k_ref, v_ref, augq_ref, augk_ref, o_ref, m_ref, acc_ref, qa_ref):
    qi = pl.program_id(2)
    ki = pl.program_id(3)
    tq = q_ref.shape[0]
    to_log2 = (FOX_HD ** -0.5) * 1.4426950408889634

    @pl.when(ki == 0)
    def _():
        m_ref[...] = jnp.full(m_ref.shape, NEG_BIG, F32)
        acc_ref[...] = jnp.zeros(acc_ref.shape, F32)
        lane = lax.broadcasted_iota(jnp.int32, (tq, LANES), 1)
        augq = augq_ref[...]
        for hh in range(HEADS_PER_STEP):
            cs = slice(hh * FOX_HD, (hh + 1) * FOX_HD)
            own = jnp.where(lane // AUG_LANES == hh, augq, jnp.zeros_like(augq))
            qa_ref[hh] = jnp.concatenate([q_ref[:, cs].astype(BF16), own], axis=1)

    def block(masked):
        augk = augk_ref[...]
        ones_col = (lax.broadcasted_iota(jnp.int32, (tq, LANES), 1) == 0).astype(BF16)
        if masked:
            visible = (lax.broadcasted_iota(jnp.int32, (tq, tq), 1) <= lax.broadcasted_iota(jnp.int32, (tq, tq), 0))
        for hh in range(HEADS_PER_STEP):
            cs = slice(hh * FOX_HD, (hh + 1) * FOX_HD)
            ka = jnp.concatenate([k_ref[:, cs].astype(BF16), augk], axis=1)
            va = jnp.concatenate([v_ref[:, cs].astype(BF16), ones_col], axis=1)
            s = _dot_nt(qa_ref[hh], ka) * to_log2
            if masked:
                s = jnp.where(visible, s, NEG_BIG)
            m_old = m_ref[hh]
            m_new = jnp.maximum(m_old, jnp.max(s, axis=-1, keepdims=True))
            p = jnp.exp2(s - m_new)
            acc_ref[hh] = jnp.exp2(m_old - m_new) * acc_ref[hh] + _dot(p.astype(BF16), va)
            m_ref[hh] = m_new

    @pl.when(ki < qi)
    def _():
        block(masked=False)

    @pl.when(ki == qi)
    def _():
        block(masked=True)
        for hh in range(HEADS_PER_STEP):
            acc = acc_ref[hh]
            o_ref[:, hh * FOX_HD:(hh + 1) * FOX_HD] = (acc[:, :FOX_HD] / acc[:, FOX_HD:FOX_HD + 1]).astype(o_ref.dtype)


def _fox_prompt(q3, k3, v3, augq, augk):
    n_batch, seq = q3.shape[0], q3.shape[1]
    nblk = seq // SEQ_BLOCK
    wcol = HEADS_PER_STEP * FOX_HD
    kv_spec = pl.BlockSpec((None, SEQ_BLOCK, wcol), lambda b, hg, qi, ki: (b, jnp.minimum(ki, qi), hg))
    return pl.pallas_call(
        _fox_prompt_kernel,
        grid=(n_batch, FOX_HEADS // HEADS_PER_STEP, nblk, nblk),
        in_specs=[
            pl.BlockSpec((None, SEQ_BLOCK, wcol), lambda b, hg, qi, ki: (b, qi, hg)),
            kv_spec, kv_spec,
            pl.BlockSpec((None, None, SEQ_BLOCK, LANES), lambda b, hg, qi, ki: (b, hg, qi, 0)),
            pl.BlockSpec((None, None, SEQ_BLOCK, LANES), lambda b, hg, qi, ki: (b, hg, jnp.minimum(ki, qi), 0)),
        ],
        out_specs=pl.BlockSpec((None, SEQ_BLOCK, wcol), lambda b, hg, qi, ki: (b, qi, hg)),
        out_shape=jax.ShapeDtypeStruct((n_batch, seq, D_MODEL), BF16),
        scratch_shapes=[pltpu.VMEM((HEADS_PER_STEP, SEQ_BLOCK, 1), F32),
                        pltpu.VMEM((HEADS_PER_STEP, SEQ_BLOCK, 2 * FOX_HD), F32),
                        pltpu.VMEM((HEADS_PER_STEP, SEQ_BLOCK, 2 * FOX_HD), BF16)],
        compiler_params=_params(("arbitrary", "arbitrary", "arbitrary", "arbitrary")),
        name="fox_prompt",
    )(q3, k3, v3, augq, augk)


HALF_HEADS = FOX_HEADS // 2
HALF_ROWS = PAGE_SIZE * HALF_HEADS


def _later_log_forget(x, carry):
    lane = lax.broadcasted_iota(jnp.int32, x.shape, 1)
    sub = lax.broadcasted_iota(jnp.int32, x.shape, 0)
    after = x
    before = x
    shift = HALF_HEADS
    while shift < LANES:
        after = after + jnp.where(lane + shift < LANES, pltpu.roll(after, LANES - shift, axis=1), 0.0)
        before = before + jnp.where(lane >= shift, pltpu.roll(before, shift, axis=1), 0.0)
        shift *= 2
    row_total = after + before - x
    below = row_total
    shift = 1
    while shift < x.shape[0]:
        below = below + jnp.where(sub + shift < x.shape[0], pltpu.roll(below, x.shape[0] - shift, axis=0), 0.0)
        shift *= 2
    later = (after - x) + (below - row_total) + carry
    return later, carry + below[0:1, :]


def _fox_sample_kernel(pt_ref, qs_ref, kn_ref, vn_ref, lfn_ref, *refs):
    g_pages = PAGES_PER_STEP
    k_pages = refs[:g_pages]
    v_pages = refs[g_pages:2 * g_pages]
    lf_pages = refs[2 * g_pages:3 * g_pages]
    o_ref = refs[3 * g_pages]
    q2_ref, acc_ref, m_ref, l_ref, lcol_ref, carry_ref = refs[3 * g_pages + 1:]
    j = pl.program_id(1)
    t = qs_ref.shape[0]
    rows = HALF_HEADS * t
    scale = FOX_HD ** -0.5
    row = lax.broadcasted_iota(jnp.int32, (rows, LANES), 0)
    lane = lax.broadcasted_iota(jnp.int32, (rows, LANES), 1)

    def online_update(hf, s, v):
        m_old = m_ref[hf]
        m_new = jnp.maximum(m_old, jnp.max(s, axis=-1, keepdims=True))
        alpha = jnp.exp(m_old - m_new)
        p = jnp.exp(s - m_new)
        l_ref[hf] = alpha * l_ref[hf] + jnp.sum(p, axis=-1, keepdims=True)
        acc_ref[hf] = alpha * acc_ref[hf] + _dot(p.astype(BF16), v)
        m_ref[hf] = m_new

    @pl.when(j == 0)
    def _():
        m_ref[...] = jnp.full(m_ref.shape, NEG_BIG, F32)
        l_ref[...] = jnp.zeros(l_ref.shape, F32)
        acc_ref[...] = jnp.zeros(acc_ref.shape, F32)
        carry_ref[...] = jnp.zeros(carry_ref.shape, F32)
        full_row = lax.broadcasted_iota(jnp.int32, (LANES, LANES), 0)
        full_lane = lax.broadcasted_iota(jnp.int32, (LANES, LANES), 1)
        lfn = jnp.where(lax.broadcasted_iota(jnp.int32, (t, LANES), 1) < FOX_HEADS, lfn_ref[...], 0.0)
        tri = (lax.broadcasted_iota(jnp.int32, (t, t), 0) >= lax.broadcasted_iota(jnp.int32, (t, t), 1))
        l_new = _dot(tri.astype(F32), lfn, precision=HIGHEST)
        l_pad = jnp.concatenate([l_new, jnp.zeros((LANES - t, LANES), F32)], axis=0)
        head_sel = (full_row // t == full_lane).astype(F32)
        l_t = _dot_nt(head_sel, l_pad, precision=HIGHEST)
        lcol = jnp.sum(jnp.where(full_lane == full_row % t, l_t, 0.0), axis=-1, keepdims=True)
        lrow = jnp.sum(jnp.where(full_lane == full_row, lcol, 0.0), axis=0, keepdims=True)
        lcol_ref[...] = lcol
        heads = range(FOX_HEADS)
        k_new = jnp.concatenate([kn_ref[:, h * FOX_HD:(h + 1) * FOX_HD] for h in heads], axis=0).astype(BF16)
        v_new = jnp.concatenate([vn_ref[:, h * FOX_HD:(h + 1) * FOX_HD] for h in heads], axis=0).astype(BF16)
        for hf in range(2):
            hs = range(hf * HALF_HEADS, (hf + 1) * HALF_HEADS)
            q2 = jnp.concatenate([qs_ref[:, h * FOX_HD:(h + 1) * FOX_HD] for h in hs], axis=0).astype(BF16)
            q2_ref[hf] = q2
            s = _dot_nt(q2, k_new) * scale + lcol[hf * rows:(hf + 1) * rows] - lrow
            visible = (lane // t == hf * HALF_HEADS + row // t) & (lane % t <= row % t)
            online_update(hf, jnp.where(visible, s, NEG_BIG), v_new)

    @pl.when(j > 0)
    def _():
        other_head = jnp.where(lane % HALF_HEADS == row // t, 0.0, NEG_BIG)
        for hf in range(2):
            hsl = slice(hf * HALF_HEADS, (hf + 1) * HALF_HEADS)
            carry = carry_ref[hf]
            bias, k_parts, v_parts = [], [], []
            for g in range(g_pages):
                k_half = k_pages[g][:, hsl, :].reshape(HALF_ROWS, FOX_HD).astype(BF16)
                v_half = v_pages[g][:, hsl, :].reshape(HALF_ROWS, FOX_HD).astype(BF16)
                k_parts.append(k_half)
                v_parts.append(v_half)
                later, carry = _later_log_forget(lf_pages[g][hf], carry)
                bias += [later[a:a + 1, :] + other_head for a in range(later.shape[0])]
            carry_ref[hf] = carry
            s = _dot_nt(q2_ref[hf], jnp.concatenate(k_parts, axis=0)) * scale + lcol_ref[hf * rows:(hf + 1) * rows, :]
            online_update(hf, s + jnp.concatenate(bias, axis=1), jnp.concatenate(v_parts, axis=0))

    @pl.when(j == pl.num_programs(1) - 1)
    def _():
        for hf in range(2):
            out = acc_ref[hf] / l_ref[hf]
            for hh in range(HALF_HEADS):
                h = hf * HALF_HEADS + hh
                o_ref[:, h * FOX_HD:(h + 1) * FOX_HD] = out[hh * t:(hh + 1) * t, :]


def _fox_sample(qs, ks, vs, lfs, cache_k, cache_v, cache_logf, page_table, n_batch, t):
    n_pages = page_table.shape[1]
    n_pool = cache_k.shape[1]
    g_pages = PAGES_PER_STEP
    steps = 1 + n_pages // g_pages
    pt_flat = page_table.reshape(-1)
    lf_halves = cache_logf[0].reshape(n_pool, PAGE_SIZE, 2, HALF_HEADS).transpose(0, 2, 1, 3)
    lf_halves = lf_halves.reshape(n_pool, 2, HALF_ROWS // LANES, LANES)

    def page_map(g, rank):
        def index_map(b, j, pt):
            idx = n_pages - 1 - (jnp.maximum(j, 1) - 1) * g_pages - g
            return (0,) * (rank - 4) + (pt[b * n_pages + idx], 0, 0, 0)
        return index_map

    row_spec = lambda w: pl.BlockSpec((t, w), lambda b, j, pt: (b, 0))
    in_specs = [row_spec(D_MODEL), row_spec(D_MODEL), row_spec(D_MODEL), row_spec(LANES)]
    kv_block = (None, None, PAGE_SIZE, FOX_HEADS, FOX_HD)
    in_specs += [pl.BlockSpec(kv_block, page_map(g, 5)) for g in range(g_pages)]
    in_specs += [pl.BlockSpec(kv_block, page_map(g, 5)) for g in range(g_pages)]
    in_specs += [pl.BlockSpec((None, 2, HALF_ROWS // LANES, LANES), page_map(g, 4)) for g in range(g_pages)]
    rows = HALF_HEADS * t
    grid_spec = pltpu.PrefetchScalarGridSpec(
        num_scalar_prefetch=1,
        grid=(n_batch, steps),
        in_specs=in_specs,
        out_specs=pl.BlockSpec((t, D_MODEL), lambda b, j, pt: (b, 0)),
        scratch_shapes=[pltpu.VMEM((2, rows, FOX_HD), BF16),
                        pltpu.VMEM((2, rows, FOX_HD), F32),
                        pltpu.VMEM((2, rows, 1), F32),
                        pltpu.VMEM((2, rows, 1), F32),
                        pltpu.VMEM((2 * rows, 1), F32),
                        pltpu.VMEM((2, 1, LANES), F32)],
    )
    return pl.pallas_call(
        _fox_sample_kernel,
        grid_spec=grid_spec,
        out_shape=jax.ShapeDtypeStruct((n_batch * t, D_MODEL), F32),
        compiler_params=_params(("arbitrary", "arbitrary")),
        name="fox_sample",
    )(pt_flat, qs, ks, vs, lfs, *([cache_k] * g_pages), *([cache_v] * g_pages), *([lf_halves] * g_pages))


def _pad_cols(w, width):
    return jnp.pad(w, ((0, 0), (0, width - w.shape[1])))


def kernel(x_prompt, x_sample, state_pool, state_gla, cache_k, cache_v, cache_logf, page_table, meta_tokens,
           norm_mix_e, w_in_e, pool_w, pool_scale, gla_w_a2, gla_b_a, gla_norm, w_out_e, norm_mix_o, w_in_o,
           fox_f_bias, w_out_o, norm_mlp, w_up, w_down, norm_final):
    n_bp, seq_p = x_prompt.shape[0], x_prompt.shape[1] + N_META
    n_bs, seq_s = x_sample.shape[0], x_sample.shape[1]
    meta = jnp.broadcast_to(meta_tokens[None].astype(x_prompt.dtype), (n_bp, N_META, D_MODEL))
    hp = jnp.concatenate([meta, x_prompt], axis=1).reshape(n_bp * seq_p, D_MODEL)
    hs = x_sample.reshape(n_bs * seq_s, D_MODEL)
    row = lambda v: v.reshape(1, -1)

    w_e = w_in_e.astype(BF16)
    w_up_bf, w_down_bf = w_up.astype(BF16), w_down.astype(BF16)
    n_main = D_POOL + 2 * HK + 2 * HV
    (zp, zxp), (zs, zxs) = _in_proj(hp, hs, row(norm_mix_e[0]), w_e, _pad_cols(w_e[0, :, n_main:], LANES),
                                    jnp.zeros((1, LANES), F32), sections=(n_main // 1024,), tm=ROW_BLOCK, tn=1024,
                                    logsig_extra=False)
    pw = pool_w[0].astype(BF16)
    ps = row(pool_scale[0])
    wa2 = jnp.pad(gla_w_a2[0], ((0, LANES - GLA_RANK), (0, 0))).astype(BF16)
    ba = row(gla_b_a[0])
    gn = row(gla_norm[0])
    pool_p, pool_buf_p = _pool_prompt(zp, pw, ps, n_bp)
    gla_p, gla_state_p = _gla_prompt(zp.reshape(n_bp, seq_p, n_main), zxp.reshape(n_bp, seq_p, LANES), wa2, ba, gn)
    pool_s, gla_s, pool_buf_s, gla_state_s = _even_sample(zs, zxs, state_pool, state_gla, pw, ps, wa2, ba, gn,
                                                          n_bs, seq_s)
    hp, hs = _out_proj(hp, hs, w_out_e.astype(BF16), [pool_p, gla_p.reshape(n_bp * seq_p, HV)], [pool_s, gla_s])
    hp, hs = _mlp(hp, hs, row(norm_mlp[0]), row(norm_final), w_up_bf, w_down_bf, layer=0, final_norm=False)

    w_o = w_in_o.astype(BF16)
    fb = jnp.pad(row(fox_f_bias[0]), ((0, 0), (0, LANES - FOX_HEADS)))
    (qp, kp, vp, lfp), (qs, ks, vs, lfs) = _in_proj(
        hp, hs, row(norm_mix_o[0]), w_o, _pad_cols(w_o[0, :, 3 * D_MODEL:], LANES), fb,
        sections=(2, 2, 2), tm=SEQ_BLOCK, tn=1024, logsig_extra=True)
    shape3 = lambda a: a.reshape(n_bp, seq_p, a.shape[-1])
    augq, augk = _forget_cumsum(shape3(lfp))
    att_p = _fox_prompt(shape3(qp), shape3(kp), shape3(vp), augq, augk)
    att_s = _fox_sample(qs, ks, vs, lfs, cache_k, cache_v, cache_logf, page_table, n_bs, seq_s)
    hp, hs = _out_proj(hp, hs, w_out_o.astype(BF16), [att_p.reshape(n_bp * seq_p, D_MODEL)], [att_s])
    yp, ys = _mlp(hp, hs, row(norm_mlp[1]), row(norm_final), w_up_bf, w_down_bf, layer=1, final_norm=True)

    y_prompt = yp.reshape(n_bp, seq_p, D_MODEL)[:, N_META:]
    y_sample = ys.reshape(n_bs, seq_s, D_MODEL)
    heads = lambda a, b, s: a.reshape(1, b, s, FOX_HEADS, FOX_HD)
    lf_out = lambda a, b, s: a[:, :FOX_HEADS].reshape(1, b, s, FOX_HEADS)
    return (y_prompt, y_sample, pool_buf_p, pool_buf_s, gla_state_p, gla_state_s,
            heads(kp, n_bp, seq_p), heads(ks, n_bs, seq_s), heads(vp, n_bp, seq_p), heads(vs, n_bs, seq_s),
            lf_out(lfp, n_bp, seq_p), lf_out(lfs, n_bs, seq_s))
```

```python
import functools

import jax
import jax.numpy as jnp
from jax import lax
from jax.experimental import pallas as pl
from jax.experimental.pallas import tpu as pltpu

F32 = jnp.float32
BF16 = jnp.bfloat16
HIGHEST = lax.Precision.HIGHEST

D_MODEL = 2048
N_META = 16
EPS = 1e-6
POOL_WINDOWS = (2, 4, 8, 16)
D_POOL = 1024
POOL_GROUP = 256
POOL_BUF = 15
GLA_HEADS = 4
GLA_DK = 128
GLA_DV = 256
GLA_RANK = 16
GLA_NORMALIZER = 16.0
FOX_HEADS = 16
FOX_HD = 128
PAGE_SIZE = 128
D_FF = 8192
HK = GLA_HEADS * GLA_DK
HV = GLA_HEADS * GLA_DV

LANES = 128
NEG_BIG = -1e30
VMEM_LIMIT = 56 * 1024 * 1024

ROW_BLOCK = 1032
SEQ_BLOCK = 688
GLA_CHUNK = 48
PAGES_PER_STEP = 8

NT_DIMS = (((1,), (1,)), ((), ()))
TN_DIMS = (((0,), (0,)), ((), ()))


def _params(semantics):
    return pltpu.CompilerParams(dimension_semantics=semantics, vmem_limit_bytes=VMEM_LIMIT)


def _dot(a, b, **kw):
    return jnp.dot(a, b, preferred_element_type=F32, **kw)


def _dot_nt(a, b, **kw):
    return lax.dot_general(a, b, NT_DIMS, preferred_element_type=F32, **kw)


def _dot_tn(a, b, **kw):
    return lax.dot_general(a, b, TN_DIMS, preferred_element_type=F32, **kw)


def _log_sigmoid(x):
    return jnp.minimum(x, 0.0) - jnp.log1p(jnp.exp(-jnp.abs(x)))


def _rms_norm(x, g):
    ms = jnp.mean(x * x, axis=-1, keepdims=True)
    return x * lax.rsqrt(ms + EPS) * g


def _in_proj_kernel(sections, logsig_extra, xp_ref, xs_ref, g_ref, w_ref, wx_ref, bx_ref, *refs):
    nsec = len(sections)
    outs_p, zxp_ref = refs[:nsec], refs[nsec]
    outs_s, zxs_ref = refs[nsec + 1:2 * nsec + 1], refs[2 * nsec + 1]
    xnp_ref, xns_ref = refs[2 * nsec + 2:]
    i = pl.program_id(0)
    j = pl.program_id(1)

    def extra(xn):
        e = _dot(xn, wx_ref[...]) + bx_ref[...]
        return _log_sigmoid(e) if logsig_extra else e

    @pl.when(j == 0)
    def _():
        xn = _rms_norm(xp_ref[...], g_ref[...]).astype(BF16)
        xnp_ref[...] = xn
        zxp_ref[...] = extra(xn)

    @pl.when((j == 0) & (i == 0))
    def _():
        xn = _rms_norm(xs_ref[...], g_ref[...]).astype(BF16)
        xns_ref[...] = xn
        zxs_ref[...] = extra(xn)

    def write(outs, x_ref):
        z = _dot(x_ref[...], w_ref[...])
        off = 0
        for o_ref, n in zip(outs, sections):
            if nsec == 1:
                o_ref[...] = z.astype(o_ref.dtype)
            else:
                @pl.when((j >= off) & (j < off + n))
                def _(o_ref=o_ref):
                    o_ref[...] = z.astype(o_ref.dtype)
            off += n

    write(outs_p, xnp_ref)

    @pl.when(i == 0)
    def _():
        write(outs_s, xns_ref)


def _in_proj(xp, xs, g, w, wx, bx, sections, tm, tn, logsig_extra):
    n_p, n_s = xp.shape[0], xs.shape[0]
    ncol = sum(sections)
    grid = (n_p // tm, ncol)
    offs = [sum(sections[:k]) for k in range(len(sections))]

    def p_map(off, n):
        return lambda i, j: (i, jnp.clip(j - off, 0, n - 1))

    def s_map(off, n):
        return lambda i, j: (0, jnp.clip(jnp.where(i == 0, j, ncol - 1) - off, 0, n - 1))

    out_shape, out_specs = [], []
    for rows, tr, mk in ((n_p, tm, p_map), (n_s, n_s, s_map)):
        for off, n in zip(offs, sections):
            out_shape.append(jax.ShapeDtypeStruct((rows, n * tn), F32))
            out_specs.append(pl.BlockSpec((tr, tn), mk(off, n)))
        out_shape.append(jax.ShapeDtypeStruct((rows, LANES), F32))
        out_specs.append(pl.BlockSpec((tr, LANES), (lambda i, j: (i, 0)) if rows == n_p else (lambda i, j: (0, 0))))
    outs = pl.pallas_call(
        functools.partial(_in_proj_kernel, tuple(sections), logsig_extra),
        grid=grid,
        in_specs=[
            pl.BlockSpec((tm, D_MODEL), lambda i, j: (i, 0)),
            pl.BlockSpec((n_s, D_MODEL), lambda i, j: (0, 0)),
            pl.BlockSpec((1, D_MODEL), lambda i, j: (0, 0)),
            pl.BlockSpec((None, D_MODEL, tn), lambda i, j: (0, 0, j)),
            pl.BlockSpec((D_MODEL, LANES), lambda i, j: (0, 0)),
            pl.BlockSpec((1, LANES), lambda i, j: (0, 0)),
        ],
        out_specs=out_specs,
        out_shape=out_shape,
        scratch_shapes=[pltpu.VMEM((tm, D_MODEL), BF16), pltpu.VMEM((n_s, D_MODEL), BF16)],
        compiler_params=_params(("arbitrary", "arbitrary")),
        name="in_proj",
    )(xp, xs, g, w, wx, bx)
    k = len(sections) + 1
    return outs[:k], outs[k:]


def _out_proj_kernel(widths, xp_ref, xs_ref, w_ref, *refs):
    na = len(widths)
    ap, a_s = refs[:na], refs[na:2 * na]
    op_ref, os_ref = refs[2 * na:]
    i = pl.program_id(0)

    def compute(x_ref, a_refs, o_ref):
        acc = x_ref[...]
        off = 0
        for a_ref, kw in zip(a_refs, widths):
            acc = acc + _dot(a_ref[...].astype(BF16), w_ref[off:off + kw, :])
            off += kw
        o_ref[...] = acc

    compute(xp_ref, ap, op_ref)

    @pl.when(i == 0)
    def _():
        compute(xs_ref, a_s, os_ref)


def _out_proj(xp, xs, w, a_p, a_s, tm=SEQ_BLOCK, tn=D_MODEL):
    n_p, n_s = xp.shape[0], xs.shape[0]
    widths = tuple(a.shape[1] for a in a_p)
    ncol = D_MODEL // tn
    grid = (n_p // tm, ncol)
    s_col = lambda i, j: (0, jnp.where(i == 0, j, ncol - 1))
    in_specs = [
        pl.BlockSpec((tm, tn), lambda i, j: (i, j)),
        pl.BlockSpec((n_s, tn), s_col),
        pl.BlockSpec((None, D_MODEL, tn), lambda i, j: (0, 0, j)),
    ]
    in_specs += [pl.BlockSpec((tm, kw), lambda i, j: (i, 0)) for kw in widths]
    in_specs += [pl.BlockSpec((n_s, kw), lambda i, j: (0, 0)) for kw in widths]
    return pl.pallas_call(
        functools.partial(_out_proj_kernel, widths),
        grid=grid,
        in_specs=in_specs,
        out_specs=[pl.BlockSpec((tm, tn), lambda i, j: (i, j)), pl.BlockSpec((n_s, tn), s_col)],
        out_shape=[jax.ShapeDtypeStruct((n_p, D_MODEL), F32), jax.ShapeDtypeStruct((n_s, D_MODEL), F32)],
        compiler_params=_params(("arbitrary", "arbitrary")),
        name="out_proj",
    )(xp, xs, w, *a_p, *a_s)


def _mlp_kernel(final_norm, xp_ref, xs_ref, g_ref, gf_ref, wu_ref, wd_ref, op_ref, os_ref, xnp_ref, xns_ref):
    i = pl.program_id(0)
    j = pl.program_id(1)
    last = pl.num_programs(1) - 1

    def step(x_ref, xn_ref, o_ref):
        @pl.when(j == 0)
        def _():
            x = x_ref[...]
            xn_ref[...] = _rms_norm(x, g_ref[...]).astype(BF16)
            o_ref[...] = x

        a = _dot(xn_ref[...], wu_ref[...])
        a = jnp.square(jnp.maximum(a, 0.0)).astype(BF16)
        o_ref[...] += _dot(a, wd_ref[...])
        if final_norm:
            @pl.when(j == last)
            def _():
                o_ref[...] = _rms_norm(o_ref[...], gf_ref[...])

    step(xp_ref, xnp_ref, op_ref)

    @pl.when(i == 0)
    def _():
        step(xs_ref, xns_ref, os_ref)


def _mlp(xp, xs, g, gf, wu, wd, layer, final_norm, tf=512):
    n_p, n_s = xp.shape[0], xs.shape[0]
    grid = (n_p // ROW_BLOCK, D_FF // tf)
    return pl.pallas_call(
        functools.partial(_mlp_kernel, final_norm),
        grid=grid,
        in_specs=[
            pl.BlockSpec((ROW_BLOCK, D_MODEL), lambda i, j: (i, 0)),
            pl.BlockSpec((n_s, D_MODEL), lambda i, j: (0, 0)),
            pl.BlockSpec((1, D_MODEL), lambda i, j: (0, 0)),
            pl.BlockSpec((1, D_MODEL), lambda i, j: (0, 0)),
            pl.BlockSpec((None, D_MODEL, tf), lambda i, j: (layer, 0, j)),
            pl.BlockSpec((None, tf, D_MODEL), lambda i, j: (layer, j, 0)),
        ],
        out_specs=[pl.BlockSpec((ROW_BLOCK, D_MODEL), lambda i, j: (i, 0)),
                   pl.BlockSpec((n_s, D_MODEL), lambda i, j: (0, 0))],
        out_shape=[jax.ShapeDtypeStruct((n_p, D_MODEL), F32), jax.ShapeDtypeStruct((n_s, D_MODEL), F32)],
        scratch_shapes=[pltpu.VMEM((ROW_BLOCK, D_MODEL), BF16), pltpu.VMEM((n_s, D_MODEL), BF16)],
        compiler_params=_params(("arbitrary", "arbitrary")),
        name="mlp",
    )(xp, xs, g, gf, wu, wd)


def _pool_groups(ext_ref, rows, count_fn, pw_ref, ps_ref, o_ref):
    for g, w in enumerate(POOL_WINDOWS):
        cs = slice(g * POOL_GROUP, (g + 1) * POOL_GROUP)
        u = ext_ref[16:16 + rows, cs]
        acc = u
        for back in range(1, w):
            acc = acc + ext_ref[16 - back:16 - back + rows, cs]
        pooled = acc / count_fn(w) - u
        mixed = _dot(pooled.astype(BF16), pw_ref[g]) * ps_ref[:, cs]
        o_ref[:, cs] = mixed.astype(o_ref.dtype)


def _pool_prompt_kernel(z_ref, pw_ref, ps_ref, o_ref, buf_ref, ext_ref):
    blk = pl.program_id(0) % (2064 // SEQ_BLOCK)

    @pl.when(blk == 0)
    def _():
        ext_ref[0:16, :] = jnp.zeros((16, D_POOL), F32)

    ext_ref[16:16 + SEQ_BLOCK, :] = z_ref[...]
    pos = blk * SEQ_BLOCK + lax.broadcasted_iota(jnp.int32, (SEQ_BLOCK, 1), 0)
    _pool_groups(ext_ref, SEQ_BLOCK, lambda w: jnp.minimum(w, pos + 1).astype(F32), pw_ref, ps_ref, o_ref)

    @pl.when(blk == 2064 // SEQ_BLOCK - 1)
    def _():
        buf_ref[...] = ext_ref[16 + SEQ_BLOCK - POOL_BUF:16 + SEQ_BLOCK, :]

    ext_ref[0:16, :] = ext_ref[SEQ_BLOCK:SEQ_BLOCK + 16, :]


def _pool_prompt(z_main, pw, ps, n_batch):
    n_p = z_main.shape[0]
    per_seq = 2064 // SEQ_BLOCK
    return pl.pallas_call(
        _pool_prompt_kernel,
        grid=(n_p // SEQ_BLOCK,),
        in_specs=[
            pl.BlockSpec((SEQ_BLOCK, D_POOL), lambda r: (r, 0)),
            pl.BlockSpec((len(POOL_WINDOWS), POOL_GROUP, POOL_GROUP), lambda r: (0, 0, 0)),
            pl.BlockSpec((1, D_POOL), lambda r: (0, 0)),
        ],
        out_specs=[pl.BlockSpec((SEQ_BLOCK, D_POOL), lambda r: (r, 0)),
                   pl.BlockSpec((None, None, POOL_BUF, D_POOL), lambda r: (0, r // per_seq, 0, 0))],
        out_shape=[jax.ShapeDtypeStruct((n_p, D_POOL), BF16),
                   jax.ShapeDtypeStruct((1, n_batch, POOL_BUF, D_POOL), F32)],
        scratch_shapes=[pltpu.VMEM((16 + SEQ_BLOCK, D_POOL), F32)],
        compiler_params=_params(("arbitrary",)),
        name="pool_prompt",
    )(z_main, pw, ps)


def _gla_chunk(qk, v, r, alr, wa2, ba, gnorm, st_get, st_set, o_set):
    c = qk.shape[0]
    x = _dot(alr.astype(BF16), wa2) + ba
    la = _log_sigmoid(x) * (1.0 / GLA_NORMALIZER)
    ri = lax.broadcasted_iota(jnp.int32, (c, c), 0)
    ci = lax.broadcasted_iota(jnp.int32, (c, c), 1)
    causal = ri >= ci
    b = _dot(causal.astype(F32), la, precision=HIGHEST)
    mid = c // 2 - 1
    bm = b[mid:mid + 1, :]
    be = b[c - 1:c, :]
    q = qk[:, :HK] * (GLA_DK ** -0.5)
    k = qk[:, HK:]
    qt = q * jnp.exp(b - bm)
    kt = k * jnp.exp(bm - b)
    qi = qt * jnp.exp(bm)
    khat = kt * jnp.exp(be - bm)
    e_end = jnp.exp(be)
    for h in range(GLA_HEADS):
        ks = slice(h * GLA_DK, (h + 1) * GLA_DK)
        vs = slice(h * GLA_DV, (h + 1) * GLA_DV)
        vh = v[:, vs].astype(BF16)
        att = _dot_nt(qt[:, ks].astype(BF16), kt[:, ks].astype(BF16))
        att = jnp.where(causal, att, 0.0)
        st = st_get(h)
        o = _dot(att.astype(BF16), vh) + _dot_nt(qi[:, ks].astype(BF16), st.astype(BF16))
        st_set(h, st * e_end[:, ks] + _dot_tn(vh, khat[:, ks].astype(BF16)))
        on = _rms_norm(o, gnorm[:, vs])
        rh = r[:, vs]
        gate = rh / (1.0 + jnp.exp(-rh))
        o_set(h, on * gate)


def _gla_prompt_kernel(n_batch, qk_ref, v_ref, r_ref, alr_ref, wa2_ref, ba_ref, gn_ref, o_ref, s_ref, st_ref):
    c = pl.program_id(0)

    @pl.when(c == 0)
    def _():
        st_ref[...] = jnp.zeros(st_ref.shape, F32)

    for bi in range(n_batch):
        def st_get(h, bi=bi):
            return st_ref[bi * GLA_HEADS + h]

        def st_set(h, val, bi=bi):
            st_ref[bi * GLA_HEADS + h] = val

        def o_set(h, val, bi=bi):
            o_ref[bi, :, h * GLA_DV:(h + 1) * GLA_DV] = val.astype(o_ref.dtype)

        _gla_chunk(qk_ref[bi], v_ref[bi], r_ref[bi], alr_ref[bi], wa2_ref[...], ba_ref[...], gn_ref[...],
                   st_get, st_set, o_set)

    @pl.when(c == pl.num_programs(0) - 1)
    def _():
        for bi in range(n_batch):
            for h in range(GLA_HEADS):
                s_ref[0, bi, h] = st_ref[bi * GLA_HEADS + h].T


def _gla_prompt(z_main3, zx3, wa2, ba, gn):
    n_batch, seq = z_main3.shape[0], z_main3.shape[1]
    blk = lambda col: pl.BlockSpec((n_batch, GLA_CHUNK, 1024), lambda c, col=col: (0, c, col))
    const2 = lambda shape: pl.BlockSpec(shape, lambda c: (0, 0))
    return pl.pallas_call(
        functools.partial(_gla_prompt_kernel, n_batch),
        grid=(seq // GLA_CHUNK,),
        in_specs=[blk(1), blk(2), blk(3),
                  pl.BlockSpec((n_batch, GLA_CHUNK, LANES), lambda c: (0, c, 0)),
                  const2((LANES, HK)), const2((1, HK)), const2((1, HV))],
        out_specs=[pl.BlockSpec((n_batch, GLA_CHUNK, HV), lambda c: (0, c, 0)),
                   pl.BlockSpec((1, n_batch, GLA_HEADS, GLA_DK, GLA_DV), lambda c: (0, 0, 0, 0, 0))],
        out_shape=[jax.ShapeDtypeStruct((n_batch, seq, HV), BF16),
                   jax.ShapeDtypeStruct((1, n_batch, GLA_HEADS, GLA_DK, GLA_DV), F32)],
        scratch_shapes=[pltpu.VMEM((n_batch * GLA_HEADS, GLA_DV, GLA_DK), F32)],
        compiler_params=_params(("arbitrary",)),
        name="gla_prompt",
    )(z_main3, z_main3, z_main3, zx3, wa2, ba, gn)


def _even_sample_kernel(u_ref, qk_ref, v_ref, r_ref, alr_ref, pbuf_ref, s0_ref, pw_ref, ps_ref, wa2_ref, ba_ref,
                        gn_ref, po_ref, go_ref, nbuf_ref, ns_ref, ext_ref):
    t = u_ref.shape[0]
    ext_ref[0:1, :] = jnp.zeros((1, D_POOL), F32)
    ext_ref[1:16, :] = pbuf_ref[...]
    ext_ref[16:16 + t, :] = u_ref[...]
    _pool_groups(ext_ref, t, lambda w: float(w), pw_ref, ps_ref, po_ref)
    nbuf_ref[...] = ext_ref[16 + t - POOL_BUF:16 + t, :]

    def st_get(h):
        return s0_ref[h].T

    def st_set(h, val):
        ns_ref[h] = val.T

    def o_set(h, val):
        go_ref[:, h * GLA_DV:(h + 1) * GLA_DV] = val

    _gla_chunk(qk_ref[...], v_ref[...], r_ref[...], alr_ref[...], wa2_ref[...], ba_ref[...], gn_ref[...],
               st_get, st_set, o_set)


def _even_sample(zs_main, zxs, pool_buf, gla_s0, pw, ps, wa2, ba, gn, n_batch, t):
    blk = lambda col: pl.BlockSpec((t, 1024), lambda b, col=col: (b, col))
    const = lambda shape: pl.BlockSpec(shape, lambda b: (0,) * len(shape))
    return pl.pallas_call(
        _even_sample_kernel,
        grid=(n_batch,),
        in_specs=[blk(0), blk(1), blk(2), blk(3),
                  pl.BlockSpec((t, LANES), lambda b: (b, 0)),
                  pl.BlockSpec((None, None, POOL_BUF, D_POOL), lambda b: (0, b, 0, 0)),
                  pl.BlockSpec((None, None, GLA_HEADS, GLA_DK, GLA_DV), lambda b: (0, b, 0, 0, 0)),
                  const((len(POOL_WINDOWS), POOL_GROUP, POOL_GROUP)), const((1, D_POOL)),
                  const((LANES, HK)), const((1, HK)), const((1, HV))],
        out_specs=[pl.BlockSpec((t, D_POOL), lambda b: (b, 0)),
                   pl.BlockSpec((t, HV), lambda b: (b, 0)),
                   pl.BlockSpec((None, None, POOL_BUF, D_POOL), lambda b: (0, b, 0, 0)),
                   pl.BlockSpec((None, None, GLA_HEADS, GLA_DK, GLA_DV), lambda b: (0, b, 0, 0, 0))],
        out_shape=[jax.ShapeDtypeStruct((n_batch * t, D_POOL), F32),
                   jax.ShapeDtypeStruct((n_batch * t, HV), F32),
                   jax.ShapeDtypeStruct((1, n_batch, POOL_BUF, D_POOL), F32),
                   jax.ShapeDtypeStruct((1, n_batch, GLA_HEADS, GLA_DK, GLA_DV), F32)],
        scratch_shapes=[pltpu.VMEM((16 + t, D_POOL), F32)],
        compiler_params=_params(("arbitrary",)),
        name="even_sample",
    )(zs_main, zs_main, zs_main, zs_main, zxs, pool_buf, gla_s0, pw, ps, wa2, ba, gn)


HEADS_PER_STEP = 8
AUG_LANES = 8

def _split3(x):
    hi = x.astype(BF16)
    r1 = x - hi.astype(F32)
    mid = r1.astype(BF16)
    lo = (r1 - mid.astype(F32)).astype(BF16)
    return hi, mid, lo


def _forget_cumsum_kernel(lf_ref, augq_ref, augk_ref, a_ref, b_ref):
    seq = lf_ref.shape[0]
    pad = a_ref.shape[0] - seq
    a_ref[0:pad, :] = jnp.zeros((pad, LANES), F32)
    b_ref[0:pad, :] = jnp.zeros((pad, LANES), F32)
    a_ref[pad:, :] = lf_ref[...]
    src, dst = a_ref, b_ref
    shift = 1
    while shift < seq:
        dst[pad:, :] = src[pad:, :] + src[pad - shift:pad - shift + seq, :]
        src, dst = dst, src
        shift *= 2
    parts = _split3(src[pad:, :] * (FOX_HD ** 0.5))
    head = lax.broadcasted_iota(jnp.int32, (LANES, LANES), 0)
    dest = lax.broadcasted_iota(jnp.int32, (LANES, LANES), 1)
    lane = lax.broadcasted_iota(jnp.int32, (seq, LANES), 1)
    used = lane < HEADS_PER_STEP * AUG_LANES
    ones_q = (used & (lane % AUG_LANES >= 3) & (lane % AUG_LANES < 6)).astype(F32)
    ones_k = (used & (lane % AUG_LANES < 3)).astype(F32)
    for hg in range(FOX_HEADS // HEADS_PER_STEP):
        from_head = (head == hg * HEADS_PER_STEP + dest // AUG_LANES) & (dest < HEADS_PER_STEP * AUG_LANES)
        ft = sum(_dot(parts[c], (from_head & (dest % AUG_LANES == c)).astype(BF16)) for c in range(3))
        fs = sum(_dot(parts[c], (from_head & (dest % AUG_LANES == 3 + c)).astype(BF16)) for c in range(3))
        augq_ref[hg] = (ft + ones_q).astype(BF16)
        augk_ref[hg] = (ones_k - fs).astype(BF16)


def _forget_cumsum(lf3):
    n_batch, seq = lf3.shape[0], lf3.shape[1]
    pad = 2048
    n_groups = FOX_HEADS // HEADS_PER_STEP
    aug_spec = pl.BlockSpec((None, n_groups, seq, LANES), lambda b: (b, 0, 0, 0))
    aug_shape = jax.ShapeDtypeStruct((n_batch, n_groups, seq, LANES), BF16)
    return pl.pallas_call(
        _forget_cumsum_kernel,
        grid=(n_batch,),
        in_specs=[pl.BlockSpec((None, seq, LANES), lambda b: (b, 0, 0))],
        out_specs=[aug_spec, aug_spec],
        out_shape=[aug_shape, aug_shape],
        scratch_shapes=[pltpu.VMEM((pad + seq, LANES), F32), pltpu.VMEM((pad + seq, LANES), F32)],
        compiler_params=_params(("arbitrary",)),
        name="forget_cumsum",
    )(lf3)


def _fox_prompt_kernel(q_ref, k_ref, v_ref, augq_ref, augk_ref, o_ref, m_ref, acc_ref, qa_ref):
    qi = pl.program_id(2)
    ki = pl.program_id(3)
    tq = q_ref.shape[0]
    to_log2 = (FOX_HD ** -0.5) * 1.4426950408889634

    @pl.when(ki == 0)
    def _():
        m_ref[...] = jnp.full(m_ref.shape, NEG_BIG, F32)
        acc_ref[...] = jnp.zeros(acc_ref.shape, F32)
        lane = lax.broadcasted_iota(jnp.int32, (tq, LANES), 1)
        augq = augq_ref[...]
        for hh in range(HEADS_PER_STEP):
            cs = slice(hh * FOX_HD, (hh + 1) * FOX_HD)
            own = jnp.where(lane // AUG_LANES == hh, augq, jnp.zeros_like(augq))
            qa_ref[hh] = jnp.concatenate([q_ref[:, cs].astype(BF16), own], axis=1)

    def update(hh, rows, ka, va, first_key_offset):
        s = _dot_nt(qa_ref[hh, rows, :], ka) * to_log2
        if first_key_offset is not None:
            ri = lax.broadcasted_iota(jnp.int32, s.shape, 0)
            ci = lax.broadcasted_iota(jnp.int32, s.shape, 1)
            s = jnp.where(ci <= ri + first_key_offset, s, NEG_BIG)
        m_old = m_ref[hh, rows, :]
        m_new = jnp.maximum(m_old, jnp.max(s, axis=-1, keepdims=True))
        p = jnp.exp2(s - m_new)
        acc_ref[hh, rows, :] = jnp.exp2(m_old - m_new) * acc_ref[hh, rows, :] + _dot(p.astype(BF16), va)
        m_ref[hh, rows, :] = m_new

    def block(diagonal):
        augk = augk_ref[...]
        ones_col = (lax.broadcasted_iota(jnp.int32, (tq, LANES), 1) == 0).astype(BF16)
        for hh in range(HEADS_PER_STEP):
            cs = slice(hh * FOX_HD, (hh + 1) * FOX_HD)
            ka = jnp.concatenate([k_ref[:, cs].astype(BF16), augk], axis=1)
            va = jnp.concatenate([v_ref[:, cs].astype(BF16), ones_col], axis=1)
            update(hh, slice(0, tq), ka, va, 0 if diagonal else None)

    @pl.when(ki < qi)
    def _():
        block(diagonal=False)

    @pl.when(ki == qi)
    def _():
        block(diagonal=True)
        for hh in range(HEADS_PER_STEP):
            acc = acc_ref[hh]
            o_ref[:, hh * FOX_HD:(hh + 1) * FOX_HD] = (acc[:, :FOX_HD] / acc[:, FOX_HD:FOX_HD + 1]).astype(o_ref.dtype)


def _fox_prompt(q3, k3, v3, augq, augk):
    n_batch, seq = q3.shape[0], q3.shape[1]
    nblk = seq // SEQ_BLOCK
    wcol = HEADS_PER_STEP * FOX_HD
    kv_spec = pl.BlockSpec((None, SEQ_BLOCK, wcol), lambda b, hg, qi, ki: (b, jnp.minimum(ki, qi), hg))
    return pl.pallas_call(
        _fox_prompt_kernel,
        grid=(n_batch, FOX_HEADS // HEADS_PER_STEP, nblk, nblk),
        in_specs=[
            pl.BlockSpec((None, SEQ_BLOCK, wcol), lambda b, hg, qi, ki: (b, qi, hg)),
            kv_spec, kv_spec,
            pl.BlockSpec((None, None, SEQ_BLOCK, LANES), lambda b, hg, qi, ki: (b, hg, qi, 0)),
            pl.BlockSpec((None, None, SEQ_BLOCK, LANES), lambda b, hg, qi, ki: (b, hg, jnp.minimum(ki, qi), 0)),
        ],
        out_specs=pl.BlockSpec((None, SEQ_BLOCK, wcol), lambda b, hg, qi, ki: (b, qi, hg)),
        out_shape=jax.ShapeDtypeStruct((n_batch, seq, D_MODEL), BF16),
        scratch_shapes=[pltpu.VMEM((HEADS_PER_STEP, SEQ_BLOCK, 1), F32),
                        pltpu.VMEM((HEADS_PER_STEP, SEQ_BLOCK, 2 * FOX_HD), F32),
                        pltpu.VMEM((HEADS_PER_STEP, SEQ_BLOCK, 2 * FOX_HD), BF16)],
        compiler_params=_params(("arbitrary", "arbitrary", "arbitrary", "arbitrary")),
        name="fox_prompt",
    )(q3, k3, v3, augq, augk)


HALF_HEADS = FOX_HEADS // 2
HALF_ROWS = PAGE_SIZE * HALF_HEADS


def _later_log_forget(x, carry):
    lane = lax.broadcasted_iota(jnp.int32, x.shape, 1)
    sub = lax.broadcasted_iota(jnp.int32, x.shape, 0)
    after = x
    before = x
    shift = HALF_HEADS
    while shift < LANES:
        after = after + jnp.where(lane + shift < LANES, pltpu.roll(after, LANES - shift, axis=1), 0.0)
        before = before + jnp.where(lane >= shift, pltpu.roll(before, shift, axis=1), 0.0)
        shift *= 2
    row_total = after + before - x
    below = row_total
    shift = 1
    while shift < x.shape[0]:
        below = below + jnp.where(sub + shift < x.shape[0], pltpu.roll(below, x.shape[0] - shift, axis=0), 0.0)
        shift *= 2
    later = (after - x) + (below - row_total) + carry
    return later, carry + below[0:1, :]


def _fox_sample_kernel(pt_ref, qs_ref, kn_ref, vn_ref, lfn_ref, *refs):
    g_pages = PAGES_PER_STEP
    k_pages = refs[:g_pages]
    v_pages = refs[g_pages:2 * g_pages]
    lf_pages = refs[2 * g_pages:3 * g_pages]
    o_ref = refs[3 * g_pages]
    q2_ref, acc_ref, m_ref, l_ref, lcol_ref, carry_ref = refs[3 * g_pages + 1:]
    j = pl.program_id(1)
    t = qs_ref.shape[0]
    rows = HALF_HEADS * t
    scale = FOX_HD ** -0.5
    row = lax.broadcasted_iota(jnp.int32, (rows, LANES), 0)
    lane = lax.broadcasted_iota(jnp.int32, (rows, LANES), 1)

    def online_update(hf, s, v):
        m_old = m_ref[hf]
        m_new = jnp.maximum(m_old, jnp.max(s, axis=-1, keepdims=True))
        alpha = jnp.exp(m_old - m_new)
        p = jnp.exp(s - m_new)
        l_ref[hf] = alpha * l_ref[hf] + jnp.sum(p, axis=-1, keepdims=True)
        acc_ref[hf] = alpha * acc_ref[hf] + _dot(p.astype(BF16), v)
        m_ref[hf] = m_new

    @pl.when(j == 0)
    def _():
        m_ref[...] = jnp.full(m_ref.shape, NEG_BIG, F32)
        l_ref[...] = jnp.zeros(l_ref.shape, F32)
        acc_ref[...] = jnp.zeros(acc_ref.shape, F32)
        carry_ref[...] = jnp.zeros(carry_ref.shape, F32)
        full_row = lax.broadcasted_iota(jnp.int32, (LANES, LANES), 0)
        full_lane = lax.broadcasted_iota(jnp.int32, (LANES, LANES), 1)
        lfn = jnp.where(lax.broadcasted_iota(jnp.int32, (t, LANES), 1) < FOX_HEADS, lfn_ref[...], 0.0)
        tri = (lax.broadcasted_iota(jnp.int32, (t, t), 0) >= lax.broadcasted_iota(jnp.int32, (t, t), 1))
        l_new = _dot(tri.astype(F32), lfn, precision=HIGHEST)
        l_pad = jnp.concatenate([l_new, jnp.zeros((LANES - t, LANES), F32)], axis=0)
        head_sel = (full_row // t == full_lane).astype(F32)
        l_t = _dot_nt(head_sel, l_pad, precision=HIGHEST)
        lcol = jnp.sum(jnp.where(full_lane == full_row % t, l_t, 0.0), axis=-1, keepdims=True)
        lrow = jnp.sum(jnp.where(full_lane == full_row, lcol, 0.0), axis=0, keepdims=True)
        lcol_ref[...] = lcol
        heads = range(FOX_HEADS)
        k_new = jnp.concatenate([kn_ref[:, h * FOX_HD:(h + 1) * FOX_HD] for h in heads], axis=0).astype(BF16)
        v_new = jnp.concatenate([vn_ref[:, h * FOX_HD:(h + 1) * FOX_HD] for h in heads], axis=0).astype(BF16)
        for hf in range(2):
            hs = range(hf * HALF_HEADS, (hf + 1) * HALF_HEADS)
            q2 = jnp.concatenate([qs_ref[:, h * FOX_HD:(h + 1) * FOX_HD] for h in hs], axis=0).astype(BF16)
            q2_ref[hf] = q2
            s = _dot_nt(q2, k_new) * scale + lcol[hf * rows:(hf + 1) * rows] - lrow
            visible = (lane // t == hf * HALF_HEADS + row // t) & (lane % t <= row % t)
            online_update(hf, jnp.where(visible, s, NEG_BIG), v_new)

    @pl.when(j > 0)
    def _():
        other_head = jnp.where(lane % HALF_HEADS == row // t, 0.0, NEG_BIG)
        for hf in range(2):
            hsl = slice(hf * HALF_HEADS, (hf + 1) * HALF_HEADS)
            carry = carry_ref[hf]
            bias, k_parts, v_parts = [], [], []
            for g in range(g_pages):
                k_half = k_pages[g][:, hsl, :].reshape(HALF_ROWS, FOX_HD).astype(BF16)
                v_half = v_pages[g][:, hsl, :].reshape(HALF_ROWS, FOX_HD).astype(BF16)
                k_parts.append(k_half)
                v_parts.append(v_half)
                later, carry = _later_log_forget(lf_pages[g][hf], carry)
                bias += [later[a:a + 1, :] + other_head for a in range(later.shape[0])]
            carry_ref[hf] = carry
            s = _dot_nt(q2_ref[hf], jnp.concatenate(k_parts, axis=0)) * scale + lcol_ref[hf * rows:(hf + 1) * rows, :]
            online_update(hf, s + jnp.concatenate(bias, axis=1), jnp.concatenate(v_parts, axis=0))

    @pl.when(j == pl.num_programs(1) - 1)
    def _():
        for hf in range(2):
            out = acc_ref[hf] / l_ref[hf]
            for hh in range(HALF_HEADS):
                h = hf * HALF_HEADS + hh
                o_ref[:, h * FOX_HD:(h + 1) * FOX_HD] = out[hh * t:(hh + 1) * t, :]


def _fox_sample(qs, ks, vs, lfs, cache_k, cache_v, cache_logf, page_table, n_batch, t):
    n_pages = page_table.shape[1]
    n_pool = cache_k.shape[1]
    g_pages = PAGES_PER_STEP
    steps = 1 + n_pages // g_pages
    pt_flat = page_table.reshape(-1)
    lf_halves = cache_logf[0].reshape(n_pool, PAGE_SIZE, 2, HALF_HEADS).transpose(0, 2, 1, 3)
    lf_halves = lf_halves.reshape(n_pool, 2, HALF_ROWS // LANES, LANES)

    def page_map(g, rank):
        def index_map(b, j, pt):
            idx = n_pages - 1 - (jnp.maximum(j, 1) - 1) * g_pages - g
            return (0,) * (rank - 4) + (pt[b * n_pages + idx], 0, 0, 0)
        return index_map

    row_spec = lambda w: pl.BlockSpec((t, w), lambda b, j, pt: (b, 0))
    in_specs = [row_spec(D_MODEL), row_spec(D_MODEL), row_spec(D_MODEL), row_spec(LANES)]
    kv_block = (None, None, PAGE_SIZE, FOX_HEADS, FOX_HD)
    in_specs += [pl.BlockSpec(kv_block, page_map(g, 5)) for g in range(g_pages)]
    in_specs += [pl.BlockSpec(kv_block, page_map(g, 5)) for g in range(g_pages)]
    in_specs += [pl.BlockSpec((None, 2, HALF_ROWS // LANES, LANES), page_map(g, 4)) for g in range(g_pages)]
    rows = HALF_HEADS * t
    grid_spec = pltpu.PrefetchScalarGridSpec(
        num_scalar_prefetch=1,
        grid=(n_batch, steps),
        in_specs=in_specs,
        out_specs=pl.BlockSpec((t, D_MODEL), lambda b, j, pt: (b, 0)),
        scratch_shapes=[pltpu.VMEM((2, rows, FOX_HD), BF16),
                        pltpu.VMEM((2, rows, FOX_HD), F32),
                        pltpu.VMEM((2, rows, 1), F32),
                        pltpu.VMEM((2, rows, 1), F32),
                        pltpu.VMEM((2 * rows, 1), F32),
                        pltpu.VMEM((2, 1, LANES), F32)],
    )
    return pl.pallas_call(
        _fox_sample_kernel,
        grid_spec=grid_spec,
        out_shape=jax.ShapeDtypeStruct((n_batch * t, D_MODEL), F32),
        compiler_params=_params(("arbitrary", "arbitrary")),
        name="fox_sample",
    )(pt_flat, qs, ks, vs, lfs, *([cache_k] * g_pages), *([cache_v] * g_pages), *([lf_halves] * g_pages))


def _pad_cols(w, width):
    return jnp.pad(w, ((0, 0), (0, width - w.shape[1])))


def kernel(x_prompt, x_sample, state_pool, state_gla, cache_k, cache_v, cache_logf, page_table, meta_tokens,
           norm_mix_e, w_in_e, pool_w, pool_scale, gla_w_a2, gla_b_a, gla_norm, w_out_e, norm_mix_o, w_in_o,
           fox_f_bias, w_out_o, norm_mlp, w_up, w_down, norm_final):
    n_bp, seq_p = x_prompt.shape[0], x_prompt.shape[1] + N_META
    n_bs, seq_s = x_sample.shape[0], x_sample.shape[1]
    meta = jnp.broadcast_to(meta_tokens[None].astype(x_prompt.dtype), (n_bp, N_META, D_MODEL))
    hp = jnp.concatenate([meta, x_prompt], axis=1).reshape(n_bp * seq_p, D_MODEL)
    hs = x_sample.reshape(n_bs * seq_s, D_MODEL)
    row = lambda v: v.reshape(1, -1)

    w_e = w_in_e.astype(BF16)
    w_up_bf, w_down_bf = w_up.astype(BF16), w_down.astype(BF16)
    n_main = D_POOL + 2 * HK + 2 * HV
    (zp, zxp), (zs, zxs) = _in_proj(hp, hs, row(norm_mix_e[0]), w_e, _pad_cols(w_e[0, :, n_main:], LANES),
                                    jnp.zeros((1, LANES), F32), sections=(n_main // 1024,), tm=ROW_BLOCK, tn=1024,
                                    logsig_extra=False)
    pw = pool_w[0].astype(BF16)
    ps = row(pool_scale[0])
    wa2 = jnp.pad(gla_w_a2[0], ((0, LANES - GLA_RANK), (0, 0))).astype(BF16)
    ba = row(gla_b_a[0])
    gn = row(gla_norm[0])
    pool_p, pool_buf_p = _pool_prompt(zp, pw, ps, n_bp)
    gla_p, gla_state_p = _gla_prompt(zp.reshape(n_bp, seq_p, n_main), zxp.reshape(n_bp, seq_p, LANES), wa2, ba, gn)
    pool_s, gla_s, pool_buf_s, gla_state_s = _even_sample(zs, zxs, state_pool, state_gla, pw, ps, wa2, ba, gn,
                                                          n_bs, seq_s)
    hp, hs = _out_proj(hp, hs, w_out_e.astype(BF16), [pool_p, gla_p.reshape(n_bp * seq_p, HV)], [pool_s, gla_s])
    hp, hs = _mlp(hp, hs, row(norm_mlp[0]), row(norm_final), w_up_bf, w_down_bf, layer=0, final_norm=False)

    w_o = w_in_o.astype(BF16)
    fb = jnp.pad(row(fox_f_bias[0]), ((0, 0), (0, LANES - FOX_HEADS)))
    (qp, kp, vp, lfp), (qs, ks, vs, lfs) = _in_proj(
        hp, hs, row(norm_mix_o[0]), w_o, _pad_cols(w_o[0, :, 3 * D_MODEL:], LANES), fb,
        sections=(2, 2, 2), tm=SEQ_BLOCK, tn=1024, logsig_extra=True)
    shape3 = lambda a: a.reshape(n_bp, seq_p, a.shape[-1])
    augq, augk = _forget_cumsum(shape3(lfp))
    att_p = _fox_prompt(shape3(qp), shape3(kp), shape3(vp), augq, augk)
    att_s = _fox_sample(qs, ks, vs, lfs, cache_k, cache_v, cache_logf, page_table, n_bs, seq_s)
    hp, hs = _out_proj(hp, hs, w_out_o.astype(BF16), [att_p.reshape(n_bp * seq_p, D_MODEL)], [att_s])
    yp, ys = _mlp(hp, hs, row(norm_mlp[1]), row(norm_final), w_up_bf, w_down_bf, layer=1, final_norm=True)

    y_prompt = yp.reshape(n_bp, seq_p, D_MODEL)[:, N_META:]
    y_sample = ys.reshape(n_bs, seq_s, D_MODEL)
    heads = lambda a, b, s: a.reshape(1, b, s, FOX_HEADS, FOX_HD)
    lf_out = lambda a, b, s: a[:, :FOX_HEADS].reshape(1, b, s, FOX_HEADS)
    return (y_prompt, y_sample, pool_buf_p, pool_buf_s, gla_state_p, gla_state_s,
            heads(kp, n_bp, seq_p), heads(ks, n_bs, seq_s), heads(vp, n_bp, seq_p), heads(vs, n_bs, seq_s),
            lf_out(lfp, n_bp, seq_p), lf_out(lfs, n_bs, seq_s))
```

```python
import functools

import jax
import jax.numpy as jnp
from jax import lax
from jax.experimental import pallas as pl
from jax.experimental.pallas import tpu as pltpu

F32 = jnp.float32
BF16 = jnp.bfloat16
HIGHEST = lax.Precision.HIGHEST

D_MODEL = 2048
N_META = 16
EPS = 1e-6
POOL_WINDOWS = (2, 4, 8, 16)
D_POOL = 1024
POOL_GROUP = 256
POOL_BUF = 15
GLA_HEADS = 4
GLA_DK = 128
GLA_DV = 256
GLA_RANK = 16
GLA_NORMALIZER = 16.0
FOX_HEADS = 16
FOX_HD = 128
PAGE_SIZE = 128
D_FF = 8192
HK = GLA_HEADS * GLA_DK
HV = GLA_HEADS * GLA_DV

LANES = 128
NEG_BIG = -1e30
VMEM_LIMIT = 56 * 1024 * 1024

ROW_BLOCK = 1032
SEQ_BLOCK = 688
GLA_CHUNK = 48
PAGES_PER_STEP = 8

NT_DIMS = (((1,), (1,)), ((), ()))
TN_DIMS = (((0,), (0,)), ((), ()))


def _params(semantics):
    return pltpu.CompilerParams(dimension_semantics=semantics, vmem_limit_bytes=VMEM_LIMIT)


def _dot(a, b, **kw):
    return jnp.dot(a, b, preferred_element_type=F32, **kw)


def _dot_nt(a, b, **kw):
    return lax.dot_general(a, b, NT_DIMS, preferred_element_type=F32, **kw)


def _dot_tn(a, b, **kw):
    return lax.dot_general(a, b, TN_DIMS, preferred_element_type=F32, **kw)


def _log_sigmoid(x):
    return jnp.minimum(x, 0.0) - jnp.log1p(jnp.exp(-jnp.abs(x)))


def _rms_norm(x, g):
    ms = jnp.mean(x * x, axis=-1, keepdims=True)
    return x * lax.rsqrt(ms + EPS) * g


BF16_ROWS = 16


def _side_cast_specs(srcs, n_steps, step_index):
    in_specs, out_specs, out_shape = [], [], []
    for arr, layer in srcs:
        rows, cols = arr.shape[1], arr.shape[2]
        blk = pl.cdiv(pl.cdiv(rows, n_steps), BF16_ROWS) * BF16_ROWS
        last = pl.cdiv(rows, blk) - 1
        in_specs.append(pl.BlockSpec(
            (None, blk, cols), lambda *g, layer=layer, last=last: (layer, jnp.minimum(step_index(*g), last), 0)))
        out_specs.append(pl.BlockSpec((blk, cols), lambda *g, last=last: (jnp.minimum(step_index(*g), last), 0)))
        out_shape.append(jax.ShapeDtypeStruct((rows, cols), BF16))
    return in_specs, out_specs, out_shape


def _side_cast(src_refs, dst_refs):
    for src_ref, dst_ref in zip(src_refs, dst_refs):
        dst_ref[...] = src_ref[...].astype(BF16)


def _in_proj_kernel(sections, logsig_extra, n_cast, xp_ref, xs_ref, g_ref, w_ref, wx_ref, bx_ref, *refs):
    nsec = len(sections)
    cast_src, refs = refs[:n_cast], refs[n_cast:]
    outs_p, zxp_ref = refs[:nsec], refs[nsec]
    outs_s, zxs_ref = refs[nsec + 1:2 * nsec + 1], refs[2 * nsec + 1]
    cast_dst = refs[2 * nsec + 2:2 * nsec + 2 + n_cast]
    xnp_ref, xns_ref = refs[2 * nsec + 2 + n_cast:]
    i = pl.program_id(0)
    j = pl.program_id(1)
    _side_cast(cast_src, cast_dst)

    def extra(xn):
        e = _dot(xn, wx_ref[...]) + bx_ref[...]
        return _log_sigmoid(e) if logsig_extra else e

    @pl.when(j == 0)
    def _():
        xn = _rms_norm(xp_ref[...], g_ref[...]).astype(BF16)
        xnp_ref[...] = xn
        zxp_ref[...] = extra(xn)

    @pl.when((j == 0) & (i == 0))
    def _():
        xn = _rms_norm(xs_ref[...], g_ref[...]).astype(BF16)
        xns_ref[...] = xn
        zxs_ref[...] = extra(xn)

    def write(outs, x_ref):
        z = _dot(x_ref[...], w_ref[...])
        off = 0
        for o_ref, n in zip(outs, sections):
            if nsec == 1:
                o_ref[...] = z.astype(o_ref.dtype)
            else:
                @pl.when((j >= off) & (j < off + n))
                def _(o_ref=o_ref):
                    o_ref[...] = z.astype(o_ref.dtype)
            off += n

    write(outs_p, xnp_ref)

    @pl.when(i == 0)
    def _():
        write(outs_s, xns_ref)


def _in_proj(xp, xs, g, w, wx, bx, sections, tm, tn, logsig_extra, casts=()):
    n_p, n_s = xp.shape[0], xs.shape[0]
    ncol = sum(sections)
    grid = (n_p // tm, ncol)
    offs = [sum(sections[:k]) for k in range(len(sections))]

    def p_map(off, n):
        return lambda i, j: (i, jnp.clip(j - off, 0, n - 1))

    def s_map(off, n):
        return lambda i, j: (0, jnp.clip(jnp.where(i == 0, j, ncol - 1) - off, 0, n - 1))

    out_shape, out_specs = [], []
    for rows, tr, mk in ((n_p, tm, p_map), (n_s, n_s, s_map)):
        for off, n in zip(offs, sections):
            out_shape.append(jax.ShapeDtypeStruct((rows, n * tn), F32))
            out_specs.append(pl.BlockSpec((tr, tn), mk(off, n)))
        out_shape.append(jax.ShapeDtypeStruct((rows, LANES), F32))
        out_specs.append(pl.BlockSpec((tr, LANES), (lambda i, j: (i, 0)) if rows == n_p else (lambda i, j: (0, 0))))
    c_in, c_out, c_shape = _side_cast_specs(casts, grid[0] * grid[1], lambda i, j: i * ncol + j)
    outs = pl.pallas_call(
        functools.partial(_in_proj_kernel, tuple(sections), logsig_extra, len(casts)),
        grid=grid,
        in_specs=[
            pl.BlockSpec((tm, D_MODEL), lambda i, j: (i, 0)),
            pl.BlockSpec((n_s, D_MODEL), lambda i, j: (0, 0)),
            pl.BlockSpec((1, D_MODEL), lambda i, j: (0, 0)),
            pl.BlockSpec((D_MODEL, tn), lambda i, j: (0, j)),
            pl.BlockSpec((D_MODEL, LANES), lambda i, j: (0, 0)),
            pl.BlockSpec((1, LANES), lambda i, j: (0, 0)),
        ] + c_in,
        out_specs=out_specs + c_out,
        out_shape=out_shape + c_shape,
        scratch_shapes=[pltpu.VMEM((tm, D_MODEL), BF16), pltpu.VMEM((n_s, D_MODEL), BF16)],
        compiler_params=_params(("arbitrary", "arbitrary")),
        name="in_proj",
    )(xp, xs, g, w, wx, bx, *[arr for arr, _ in casts])
    k = len(sections) + 1
    return outs[:k], outs[k:2 * k], outs[2 * k:]


def _out_proj_kernel(widths, xp_ref, xs_ref, w_ref, *refs):
    na = len(widths)
    ap, a_s = refs[:na], refs[na:2 * na]
    op_ref, os_ref = refs[2 * na:]
    i = pl.program_id(0)

    def compute(x_ref, a_refs, o_ref):
        acc = x_ref[...]
        off = 0
        for a_ref, kw in zip(a_refs, widths):
            acc = acc + _dot(a_ref[...].astype(BF16), w_ref[off:off + kw, :])
            off += kw
        o_ref[...] = acc

    compute(xp_ref, ap, op_ref)

    @pl.when(i == 0)
    def _():
        compute(xs_ref, a_s, os_ref)


def _out_proj(xp, xs, w, a_p, a_s, tm=SEQ_BLOCK, tn=D_MODEL):
    n_p, n_s = xp.shape[0], xs.shape[0]
    widths = tuple(a.shape[1] for a in a_p)
    ncol = D_MODEL // tn
    grid = (n_p // tm, ncol)
    s_col = lambda i, j: (0, jnp.where(i == 0, j, ncol - 1))
    in_specs = [
        pl.BlockSpec((tm, tn), lambda i, j: (i, j)),
        pl.BlockSpec((n_s, tn), s_col),
        pl.BlockSpec((D_MODEL, tn), lambda i, j: (0, j)),
    ]
    in_specs += [pl.BlockSpec((tm, kw), lambda i, j: (i, 0)) for kw in widths]
    in_specs += [pl.BlockSpec((n_s, kw), lambda i, j: (0, 0)) for kw in widths]
    return pl.pallas_call(
        functools.partial(_out_proj_kernel, widths),
        grid=grid,
        in_specs=in_specs,
        out_specs=[pl.BlockSpec((tm, tn), lambda i, j: (i, j)), pl.BlockSpec((n_s, tn), s_col)],
        out_shape=[jax.ShapeDtypeStruct((n_p, D_MODEL), F32), jax.ShapeDtypeStruct((n_s, D_MODEL), F32)],
        compiler_params=_params(("arbitrary", "arbitrary")),
        name="out_proj",
    )(xp, xs, w, *a_p, *a_s)


def _mlp_kernel(final_norm, n_cast, xp_ref, xs_ref, g_ref, gf_ref, wu_ref, wd_ref, *refs):
    cast_src, (op_ref, os_ref) = refs[:n_cast], refs[n_cast:n_cast + 2]
    cast_dst = refs[n_cast + 2:2 * n_cast + 2]
    xnp_ref, xns_ref = refs[2 * n_cast + 2:]
    i = pl.program_id(0)
    j = pl.program_id(1)
    last = pl.num_programs(1) - 1
    _side_cast(cast_src, cast_dst)

    def step(x_ref, xn_ref, o_ref):
        @pl.when(j == 0)
        def _():
            x = x_ref[...]
            xn_ref[...] = _rms_norm(x, g_ref[...]).astype(BF16)
            o_ref[...] = x

        a = _dot(xn_ref[...], wu_ref[...])
        a = jnp.square(jnp.maximum(a, 0.0)).astype(BF16)
        o_ref[...] += _dot(a, wd_ref[...])
        if final_norm:
            @pl.when(j == last)
            def _():
                o_ref[...] = _rms_norm(o_ref[...], gf_ref[...])

    step(xp_ref, xnp_ref, op_ref)

    @pl.when(i == 0)
    def _():
        step(xs_ref, xns_ref, os_ref)


def _mlp(xp, xs, g, gf, wu, wd, final_norm, casts=(), tf=512):
    n_p, n_s = xp.shape[0], xs.shape[0]
    grid = (n_p // ROW_BLOCK, D_FF // tf)
    c_in, c_out, c_shape = _side_cast_specs(casts, grid[0] * grid[1], lambda i, j: i * grid[1] + j)
    outs = pl.pallas_call(
        functools.partial(_mlp_kernel, final_norm, len(casts)),
        grid=grid,
        in_specs=[
            pl.BlockSpec((ROW_BLOCK, D_MODEL), lambda i, j: (i, 0)),
            pl.BlockSpec((n_s, D_MODEL), lambda i, j: (0, 0)),
            pl.BlockSpec((1, D_MODEL), lambda i, j: (0, 0)),
            pl.BlockSpec((1, D_MODEL), lambda i, j: (0, 0)),
            pl.BlockSpec((D_MODEL, tf), lambda i, j: (0, j)),
            pl.BlockSpec((tf, D_MODEL), lambda i, j: (j, 0)),
        ] + c_in,
        out_specs=[pl.BlockSpec((ROW_BLOCK, D_MODEL), lambda i, j: (i, 0)),
                   pl.BlockSpec((n_s, D_MODEL), lambda i, j: (0, 0))] + c_out,
        out_shape=[jax.ShapeDtypeStruct((n_p, D_MODEL), F32), jax.ShapeDtypeStruct((n_s, D_MODEL), F32)] + c_shape,
        scratch_shapes=[pltpu.VMEM((ROW_BLOCK, D_MODEL), BF16), pltpu.VMEM((n_s, D_MODEL), BF16)],
        compiler_params=_params(("arbitrary", "arbitrary")),
        name="mlp",
    )(xp, xs, g, gf, wu, wd, *[arr for arr, _ in casts])
    return outs[0], outs[1], outs[2:]


def _pool_groups(ext_ref, rows, count_fn, pw_ref, ps_ref, o_ref):
    for g, w in enumerate(POOL_WINDOWS):
        cs = slice(g * POOL_GROUP, (g + 1) * POOL_GROUP)
        u = ext_ref[16:16 + rows, cs]
        acc = u
        for back in range(1, w):
            acc = acc + ext_ref[16 - back:16 - back + rows, cs]
        pooled = acc / count_fn(w) - u
        mixed = _dot(pooled.astype(BF16), pw_ref[g]) * ps_ref[:, cs]
        o_ref[:, cs] = mixed.astype(o_ref.dtype)


def _pool_prompt_kernel(z_ref, pw_ref, ps_ref, o_ref, buf_ref, ext_ref):
    blk = pl.program_id(0) % (2064 // SEQ_BLOCK)

    @pl.when(blk == 0)
    def _():
        ext_ref[0:16, :] = jnp.zeros((16, D_POOL), F32)

    ext_ref[16:16 + SEQ_BLOCK, :] = z_ref[...]
    pos = blk * SEQ_BLOCK + lax.broadcasted_iota(jnp.int32, (SEQ_BLOCK, 1), 0)
    _pool_groups(ext_ref, SEQ_BLOCK, lambda w: jnp.minimum(w, pos + 1).astype(F32), pw_ref, ps_ref, o_ref)

    @pl.when(blk == 2064 // SEQ_BLOCK - 1)
    def _():
        buf_ref[...] = ext_ref[16 + SEQ_BLOCK - POOL_BUF:16 + SEQ_BLOCK, :]

    ext_ref[0:16, :] = ext_ref[SEQ_BLOCK:SEQ_BLOCK + 16, :]


def _pool_prompt(z_main, pw, ps, n_batch):
    n_p = z_main.shape[0]
    per_seq = 2064 // SEQ_BLOCK
    return pl.pallas_call(
        _pool_prompt_kernel,
        grid=(n_p // SEQ_BLOCK,),
        in_specs=[
            pl.BlockSpec((SEQ_BLOCK, D_POOL), lambda r: (r, 0)),
            pl.BlockSpec((len(POOL_WINDOWS), POOL_GROUP, POOL_GROUP), lambda r: (0, 0, 0)),
            pl.BlockSpec((1, D_POOL), lambda r: (0, 0)),
        ],
        out_specs=[pl.BlockSpec((SEQ_BLOCK, D_POOL), lambda r: (r, 0)),
                   pl.BlockSpec((None, None, POOL_BUF, D_POOL), lambda r: (0, r // per_seq, 0, 0))],
        out_shape=[jax.ShapeDtypeStruct((n_p, D_POOL), BF16),
                   jax.ShapeDtypeStruct((1, n_batch, POOL_BUF, D_POOL), F32)],
        scratch_shapes=[pltpu.VMEM((16 + SEQ_BLOCK, D_POOL), F32)],
        compiler_params=_params(("arbitrary",)),
        name="pool_prompt",
    )(z_main, pw, ps)


def _gla_chunk(qk, v, r, alr, wa2, ba, gnorm, st_get, st_set, o_set):
    c = qk.shape[0]
    x = _dot(alr.astype(BF16), wa2) + ba
    la = _log_sigmoid(x) * (1.0 / GLA_NORMALIZER)
    ri = lax.broadcasted_iota(jnp.int32, (c, c), 0)
    ci = lax.broadcasted_iota(jnp.int32, (c, c), 1)
    causal = ri >= ci
    b = _dot(causal.astype(F32), la, precision=HIGHEST)
    mid = c // 2 - 1
    bm = b[mid:mid + 1, :]
    be = b[c - 1:c, :]
    q = qk[:, :HK] * (GLA_DK ** -0.5)
    k = qk[:, HK:]
    qt = q * jnp.exp(b - bm)
    kt = k * jnp.exp(bm - b)
    qi = qt * jnp.exp(bm)
    khat = kt * jnp.exp(be - bm)
    e_end = jnp.exp(be)
    for h in range(GLA_HEADS):
        ks = slice(h * GLA_DK, (h + 1) * GLA_DK)
        vs = slice(h * GLA_DV, (h + 1) * GLA_DV)
        vh = v[:, vs].astype(BF16)
        att = _dot_nt(qt[:, ks].astype(BF16), kt[:, ks].astype(BF16))
        att = jnp.where(causal, att, 0.0)
        st = st_get(h)
        o = _dot(att.astype(BF16), vh) + _dot_nt(qi[:, ks].astype(BF16), st.astype(BF16))
        st_set(h, st * e_end[:, ks] + _dot_tn(vh, khat[:, ks].astype(BF16)))
        on = _rms_norm(o, gnorm[:, vs])
        rh = r[:, vs]
        gate = rh / (1.0 + jnp.exp(-rh))
        o_set(h, on * gate)


def _gla_prompt_kernel(n_batch, n_cast, qk_ref, v_ref, r_ref, alr_ref, wa2_ref, ba_ref, gn_ref, *refs):
    cast_src, (o_ref, s_ref) = refs[:n_cast], refs[n_cast:n_cast + 2]
    cast_dst, st_ref = refs[n_cast + 2:2 * n_cast + 2], refs[2 * n_cast + 2]
    c = pl.program_id(0)
    _side_cast(cast_src, cast_dst)

    @pl.when(c == 0)
    def _():
        st_ref[...] = jnp.zeros(st_ref.shape, F32)

    for bi in range(n_batch):
        def st_get(h, bi=bi):
            return st_ref[bi * GLA_HEADS + h]

        def st_set(h, val, bi=bi):
            st_ref[bi * GLA_HEADS + h] = val

        def o_set(h, val, bi=bi):
            o_ref[bi, :, h * GLA_DV:(h + 1) * GLA_DV] = val.astype(o_ref.dtype)

        _gla_chunk(qk_ref[bi], v_ref[bi], r_ref[bi], alr_ref[bi], wa2_ref[...], ba_ref[...], gn_ref[...],
                   st_get, st_set, o_set)

    @pl.when(c == pl.num_programs(0) - 1)
    def _():
        for bi in range(n_batch):
            for h in range(GLA_HEADS):
                s_ref[0, bi, h] = st_ref[bi * GLA_HEADS + h].T


def _gla_prompt(z_main3, zx3, wa2, ba, gn, casts=()):
    n_batch, seq = z_main3.shape[0], z_main3.shape[1]
    c_in, c_out, c_shape = _side_cast_specs(casts, seq // GLA_CHUNK, lambda c: c)
    blk = lambda col: pl.BlockSpec((n_batch, GLA_CHUNK, 1024), lambda c, col=col: (0, c, col))
    const2 = lambda shape: pl.BlockSpec(shape, lambda c: (0, 0))
    outs = pl.pallas_call(
        functools.partial(_gla_prompt_kernel, n_batch, len(casts)),
        grid=(seq // GLA_CHUNK,),
        in_specs=[blk(1), blk(2), blk(3),
                  pl.BlockSpec((n_batch, GLA_CHUNK, LANES), lambda c: (0, c, 0)),
                  const2((LANES, HK)), const2((1, HK)), const2((1, HV))] + c_in,
        out_specs=[pl.BlockSpec((n_batch, GLA_CHUNK, HV), lambda c: (0, c, 0)),
                   pl.BlockSpec((1, n_batch, GLA_HEADS, GLA_DK, GLA_DV), lambda c: (0, 0, 0, 0, 0))] + c_out,
        out_shape=[jax.ShapeDtypeStruct((n_batch, seq, HV), BF16),
                   jax.ShapeDtypeStruct((1, n_batch, GLA_HEADS, GLA_DK, GLA_DV), F32)] + c_shape,
        scratch_shapes=[pltpu.VMEM((n_batch * GLA_HEADS, GLA_DV, GLA_DK), F32)],
        compiler_params=_params(("arbitrary",)),
        name="gla_prompt",
    )(z_main3, z_main3, z_main3, zx3, wa2, ba, gn, *[arr for arr, _ in casts])
    return outs[0], outs[1], outs[2:]


def _even_sample_kernel(u_ref, qk_ref, v_ref, r_ref, alr_ref, pbuf_ref, s0_ref, pw_ref, ps_ref, wa2_ref, ba_ref,
                        gn_ref, po_ref, go_ref, nbuf_ref, ns_ref, ext_ref):
    t = u_ref.shape[0]
    ext_ref[0:1, :] = jnp.zeros((1, D_POOL), F32)
    ext_ref[1:16, :] = pbuf_ref[...]
    ext_ref[16:16 + t, :] = u_ref[...]
    _pool_groups(ext_ref, t, lambda w: float(w), pw_ref, ps_ref, po_ref)
    nbuf_ref[...] = ext_ref[16 + t - POOL_BUF:16 + t, :]

    def st_get(h):
        return s0_ref[h].T

    def st_set(h, val):
        ns_ref[h] = val.T

    def o_set(h, val):
        go_ref[:, h * GLA_DV:(h + 1) * GLA_DV] = val

    _gla_chunk(qk_ref[...], v_ref[...], r_ref[...], alr_ref[...], wa2_ref[...], ba_ref[...], gn_ref[...],
               st_get, st_set, o_set)


def _even_sample(zs_main, zxs, pool_buf, gla_s0, pw, ps, wa2, ba, gn, n_batch, t):
    blk = lambda col: pl.BlockSpec((t, 1024), lambda b, col=col: (b, col))
    const = lambda shape: pl.BlockSpec(shape, lambda b: (0,) * len(shape))
    return pl.pallas_call(
        _even_sample_kernel,
        grid=(n_batch,),
        in_specs=[blk(0), blk(1), blk(2), blk(3),
                  pl.BlockSpec((t, LANES), lambda b: (b, 0)),
                  pl.BlockSpec((None, None, POOL_BUF, D_POOL), lambda b: (0, b, 0, 0)),
                  pl.BlockSpec((None, None, GLA_HEADS, GLA_DK, GLA_DV), lambda b: (0, b, 0, 0, 0)),
                  const((len(POOL_WINDOWS), POOL_GROUP, POOL_GROUP)), const((1, D_POOL)),
                  const((LANES, HK)), const((1, HK)), const((1, HV))],
        out_specs=[pl.BlockSpec((t, D_POOL), lambda b: (b, 0)),
                   pl.BlockSpec((t, HV), lambda b: (b, 0)),
                   pl.BlockSpec((None, None, POOL_BUF, D_POOL), lambda b: (0, b, 0, 0)),
                   pl.BlockSpec((None, None, GLA_HEADS, GLA_DK, GLA_DV), lambda b: (0, b, 0, 0, 0))],
        out_shape=[jax.ShapeDtypeStruct((n_batch * t, D_POOL), F32),
                   jax.ShapeDtypeStruct((n_batch * t, HV), F32),
                   jax.ShapeDtypeStruct((1, n_batch, POOL_BUF, D_POOL), F32),
                   jax.ShapeDtypeStruct((1, n_batch, GLA_HEADS, GLA_DK, GLA_DV), F32)],
        scratch_shapes=[pltpu.VMEM((16 + t, D_POOL), F32)],
        compiler_params=_params(("arbitrary",)),
        name="even_sample",
    )(zs_main, zs_main, zs_main, zs_main, zxs, pool_buf, gla_s0, pw, ps, wa2, ba, gn)


HEADS_PER_STEP = 8
AUG_LANES = 8

def _split3(x):
    hi = x.astype(BF16)
    r1 = x - hi.astype(F32)
    mid = r1.astype(BF16)
    lo = (r1 - mid.astype(F32)).astype(BF16)
    return hi, mid, lo


def _forget_cumsum_kernel(lf_ref, augq_ref, augk_ref, a_ref, b_ref):
    seq = lf_ref.shape[0]
    pad = a_ref.shape[0] - seq
    a_ref[0:pad, :] = jnp.zeros((pad, LANES), F32)
    b_ref[0:pad, :] = jnp.zeros((pad, LANES), F32)
    a_ref[pad:, :] = lf_ref[...]
    src, dst = a_ref, b_ref
    shift = 1
    while shift < seq:
        dst[pad:, :] = src[pad:, :] + src[pad - shift:pad - shift + seq, :]
        src, dst = dst, src
        shift *= 2
    parts = _split3(src[pad:, :] * (FOX_HD ** 0.5))
    head = lax.broadcasted_iota(jnp.int32, (LANES, LANES), 0)
    dest = lax.broadcasted_iota(jnp.int32, (LANES, LANES), 1)
    lane = lax.broadcasted_iota(jnp.int32, (seq, LANES), 1)
    used = lane < HEADS_PER_STEP * AUG_LANES
    ones_q = (used & (lane % AUG_LANES >= 3) & (lane % AUG_LANES < 6)).astype(F32)
    ones_k = (used & (lane % AUG_LANES < 3)).astype(F32)
    for hg in range(FOX_HEADS // HEADS_PER_STEP):
        from_head = (head == hg * HEADS_PER_STEP + dest // AUG_LANES) & (dest < HEADS_PER_STEP * AUG_LANES)
        ft = sum(_dot(parts[c], (from_head & (dest % AUG_LANES == c)).astype(BF16)) for c in range(3))
        fs = sum(_dot(parts[c], (from_head & (dest % AUG_LANES == 3 + c)).astype(BF16)) for c in range(3))
        augq_ref[hg] = (ft + ones_q).astype(BF16)
        augk_ref[hg] = (ones_k - fs).astype(BF16)


def _forget_cumsum(lf3):
    n_batch, seq = lf3.shape[0], lf3.shape[1]
    pad = 2048
    n_groups = FOX_HEADS // HEADS_PER_STEP
    aug_spec = pl.BlockSpec((None, n_groups, seq, LANES), lambda b: (b, 0, 0, 0))
    aug_shape = jax.ShapeDtypeStruct((n_batch, n_groups, seq, LANES), BF16)
    return pl.pallas_call(
        _forget_cumsum_kernel,
        grid=(n_batch,),
        in_specs=[pl.BlockSpec((None, seq, LANES), lambda b: (b, 0, 0))],
        out_specs=[aug_spec, aug_spec],
        out_shape=[aug_shape, aug_shape],
        scratch_shapes=[pltpu.VMEM((pad + seq, LANES), F32), pltpu.VMEM((pad + seq, LANES), F32)],
        compiler_params=_params(("arbitrary",)),
        name="forget_cumsum",
    )(lf3)


def _fox_prompt_kernel(q_ref, k_ref, v_ref, augq_ref, augk_ref, o_ref, m_ref, acc_ref, qa_ref):
    qi = pl.program_id(2)
    ki = pl.program_id(3)
    tq = q_ref.shape[0]
    to_log2 = (FOX_HD ** -0.5) * 1.4426950408889634

    @pl.when(ki == 0)
    def _():
        m_ref[...] = jnp.full(m_ref.shape, NEG_BIG, F32)
        acc_ref[...] = jnp.zeros(acc_ref.shape, F32)
        lane = lax.broadcasted_iota(jnp.int32, (tq, LANES), 1)
        augq = augq_ref[...]
        for hh in range(HEADS_PER_STEP):
            cs = slice(hh * FOX_HD, (hh + 1) * FOX_HD)
            own = jnp.where(lane // AUG_LANES == hh, augq, jnp.zeros_like(augq))
            qa_ref[hh] = jnp.concatenate([q_ref[:, cs].astype(BF16), own], axis=1)

    def block(diagonal):
        augk = augk_ref[...]
        ones_col = (lax.broadcasted_iota(jnp.int32, (tq, LANES), 1) == 0).astype(BF16)
        if diagonal:
            visible = (lax.broadcasted_iota(jnp.int32, (tq, tq), 1) <= lax.broadcasted_iota(jnp.int32, (tq, tq), 0))

        def logits(hh):
            cs = slice(hh * FOX_HD, (hh + 1) * FOX_HD)
            ka = jnp.concatenate([k_ref[:, cs].astype(BF16), augk], axis=1)
            s = _dot_nt(qa_ref[hh], ka) * to_log2
            return jnp.where(visible, s, NEG_BIG) if diagonal else s

        s_next = logits(0)
        for hh in range(HEADS_PER_STEP):
            s = s_next
            if hh + 1 < HEADS_PER_STEP:
                s_next = logits(hh + 1)
            cs = slice(hh * FOX_HD, (hh + 1) * FOX_HD)
            va = jnp.concatenate([v_ref[:, cs].astype(BF16), ones_col], axis=1)
            m_old = m_ref[hh]
            m_new = jnp.maximum(m_old, jnp.max(s, axis=-1, keepdims=True))
            p = jnp.exp2(s - m_new)
            acc_ref[hh] = jnp.exp2(m_old - m_new) * acc_ref[hh] + _dot(p.astype(BF16), va)
            m_ref[hh] = m_new

    @pl.when(ki < qi)
    def _():
        block(diagonal=False)

    @pl.when(ki == qi)
    def _():
        block(diagonal=True)
        for hh in range(HEADS_PER_STEP):
            acc = acc_ref[hh]
            o_ref[:, hh * FOX_HD:(hh + 1) * FOX_HD] = (acc[:, :FOX_HD] / acc[:, FOX_HD:FOX_HD + 1]).astype(o_ref.dtype)


def _fox_prompt(q3, k3, v3, augq, augk):
    n_batch, seq = q3.shape[0], q3.shape[1]
    nblk = seq // SEQ_BLOCK
    wcol = HEADS_PER_STEP * FOX_HD
    kv_spec = pl.BlockSpec((None, SEQ_BLOCK, wcol), lambda b, hg, qi, ki: (b, jnp.minimum(ki, qi), hg))
    return pl.pallas_call(
        _fox_prompt_kernel,
        grid=(n_batch, FOX_HEADS // HEADS_PER_STEP, nblk, nblk),
        in_specs=[
            pl.BlockSpec((None, SEQ_BLOCK, wcol), lambda b, hg, qi, ki: (b, qi, hg)),
            kv_spec, kv_spec,
            pl.BlockSpec((None, None, SEQ_BLOCK, LANES), lambda b, hg, qi, ki: (b, hg, qi, 0)),
            pl.BlockSpec((None, None, SEQ_BLOCK, LANES), lambda b, hg, qi, ki: (b, hg, jnp.minimum(ki, qi), 0)),
        ],
        out_specs=pl.BlockSpec((None, SEQ_BLOCK, wcol), lambda b, hg, qi, ki: (b, qi, hg)),
        out_shape=jax.ShapeDtypeStruct((n_batch, seq, D_MODEL), BF16),
        scratch_shapes=[pltpu.VMEM((HEADS_PER_STEP, SEQ_BLOCK, 1), F32),
                        pltpu.VMEM((HEADS_PER_STEP, SEQ_BLOCK, 2 * FOX_HD), F32),
                        pltpu.VMEM((HEADS_PER_STEP, SEQ_BLOCK, 2 * FOX_HD), BF16)],
        compiler_params=_params(("arbitrary", "arbitrary", "arbitrary", "arbitrary")),
        name="fox_prompt",
    )(q3, k3, v3, augq, augk)


HALF_HEADS = FOX_HEADS // 2
HALF_ROWS = PAGE_SIZE * HALF_HEADS


def _later_log_forget(x, carry):
    lane = lax.broadcasted_iota(jnp.int32, x.shape, 1)
    sub = lax.broadcasted_iota(jnp.int32, x.shape, 0)
    after = x
    before = x
    shift = HALF_HEADS
    while shift < LANES:
        after = after + jnp.where(lane + shift < LANES, pltpu.roll(after, LANES - shift, axis=1), 0.0)
        before = before + jnp.where(lane >= shift, pltpu.roll(before, shift, axis=1), 0.0)
        shift *= 2
    row_total = after + before - x
    below = row_total
    shift = 1
    while shift < x.shape[0]:
        below = below + jnp.where(sub + shift < x.shape[0], pltpu.roll(below, x.shape[0] - shift, axis=0), 0.0)
        shift *= 2
    later = (after - x) + (below - row_total) + carry
    return later, carry + below[0:1, :]


def _fox_sample_kernel(pt_ref, qs_ref, kn_ref, vn_ref, lfn_ref, *refs):
    g_pages = PAGES_PER_STEP
    k_pages = refs[:g_pages]
    v_pages = refs[g_pages:2 * g_pages]
    lf_pages = refs[2 * g_pages:3 * g_pages]
    o_ref = refs[3 * g_pages]
    q2_ref, acc_ref, m_ref, l_ref, lcol_ref, carry_ref = refs[3 * g_pages + 1:]
    j = pl.program_id(1)
    t = qs_ref.shape[0]
    rows = HALF_HEADS * t
    scale = FOX_HD ** -0.5
    row = lax.broadcasted_iota(jnp.int32, (rows, LANES), 0)
    lane = lax.broadcasted_iota(jnp.int32, (rows, LANES), 1)

    def online_update(hf, s, v):
        m_old = m_ref[hf]
        m_new = jnp.maximum(m_old, jnp.max(s, axis=-1, keepdims=True))
        alpha = jnp.exp(m_old - m_new)
        p = jnp.exp(s - m_new)
        l_ref[hf] = alpha * l_ref[hf] + jnp.sum(p, axis=-1, keepdims=True)
        acc_ref[hf] = alpha * acc_ref[hf] + _dot(p.astype(BF16), v)
        m_ref[hf] = m_new

    @pl.when(j == 0)
    def _():
        m_ref[...] = jnp.full(m_ref.shape, NEG_BIG, F32)
        l_ref[...] = jnp.zeros(l_ref.shape, F32)
        acc_ref[...] = jnp.zeros(acc_ref.shape, F32)
        carry_ref[...] = jnp.zeros(carry_ref.shape, F32)
        full_row = lax.broadcasted_iota(jnp.int32, (LANES, LANES), 0)
        full_lane = lax.broadcasted_iota(jnp.int32, (LANES, LANES), 1)
        lfn = jnp.where(lax.broadcasted_iota(jnp.int32, (t, LANES), 1) < FOX_HEADS, lfn_ref[...], 0.0)
        tri = (lax.broadcasted_iota(jnp.int32, (t, t), 0) >= lax.broadcasted_iota(jnp.int32, (t, t), 1))
        l_new = _dot(tri.astype(F32), lfn, precision=HIGHEST)
        l_pad = jnp.concatenate([l_new, jnp.zeros((LANES - t, LANES), F32)], axis=0)
        head_sel = (full_row // t == full_lane).astype(F32)
        l_t = _dot_nt(head_sel, l_pad, precision=HIGHEST)
        lcol = jnp.sum(jnp.where(full_lane == full_row % t, l_t, 0.0), axis=-1, keepdims=True)
        lrow = jnp.sum(jnp.where(full_lane == full_row, lcol, 0.0), axis=0, keepdims=True)
        lcol_ref[...] = lcol
        heads = range(FOX_HEADS)
        k_new = jnp.concatenate([kn_ref[:, h * FOX_HD:(h + 1) * FOX_HD] for h in heads], axis=0).astype(BF16)
        v_new = jnp.concatenate([vn_ref[:, h * FOX_HD:(h + 1) * FOX_HD] for h in heads], axis=0).astype(BF16)
        for hf in range(2):
            hs = range(hf * HALF_HEADS, (hf + 1) * HALF_HEADS)
            q2 = jnp.concatenate([qs_ref[:, h * FOX_HD:(h + 1) * FOX_HD] for h in hs], axis=0).astype(BF16)
            q2_ref[hf] = q2
            s = _dot_nt(q2, k_new) * scale + lcol[hf * rows:(hf + 1) * rows] - lrow
            visible = (lane // t == hf * HALF_HEADS + row // t) & (lane % t <= row % t)
            online_update(hf, jnp.where(visible, s, NEG_BIG), v_new)

    @pl.when(j > 0)
    def _():
        other_head = jnp.where(lane % HALF_HEADS == row // t, 0.0, NEG_BIG)
        for hf in range(2):
            hsl = slice(hf * HALF_HEADS, (hf + 1) * HALF_HEADS)
            carry = carry_ref[hf]
            bias, k_parts, v_parts = [], [], []
            for g in range(g_pages):
                k_half = k_pages[g][:, hsl, :].reshape(HALF_ROWS, FOX_HD).astype(BF16)
                v_half = v_pages[g][:, hsl, :].reshape(HALF_ROWS, FOX_HD).astype(BF16)
                k_parts.append(k_half)
                v_parts.append(v_half)
                later, carry = _later_log_forget(lf_pages[g][hf], carry)
                bias += [later[a:a + 1, :] + other_head for a in range(later.shape[0])]
            carry_ref[hf] = carry
            s = _dot_nt(q2_ref[hf], jnp.concatenate(k_parts, axis=0)) * scale + lcol_ref[hf * rows:(hf + 1) * rows, :]
            online_update(hf, s + jnp.concatenate(bias, axis=1), jnp.concatenate(v_parts, axis=0))

    @pl.when(j == pl.num_programs(1) - 1)
    def _():
        for hf in range(2):
            out = acc_ref[hf] / l_ref[hf]
            for hh in range(HALF_HEADS):
                h = hf * HALF_HEADS + hh
                o_ref[:, h * FOX_HD:(h + 1) * FOX_HD] = out[hh * t:(hh + 1) * t, :]


def _fox_sample(qs, ks, vs, lfs, cache_k, cache_v, cache_logf, page_table, n_batch, t):
    n_pages = page_table.shape[1]
    n_pool = cache_k.shape[1]
    g_pages = PAGES_PER_STEP
    steps = 1 + n_pages // g_pages
    pt_flat = page_table.reshape(-1)
    lf_halves = cache_logf[0].reshape(n_pool, PAGE_SIZE, 2, HALF_HEADS).transpose(0, 2, 1, 3)
    lf_halves = lf_halves.reshape(n_pool, 2, HALF_ROWS // LANES, LANES)

    def page_map(g, rank):
        def index_map(b, j, pt):
            idx = n_pages - 1 - (jnp.maximum(j, 1) - 1) * g_pages - g
            return (0,) * (rank - 4) + (pt[b * n_pages + idx], 0, 0, 0)
        return index_map

    row_spec = lambda w: pl.BlockSpec((t, w), lambda b, j, pt: (b, 0))
    in_specs = [row_spec(D_MODEL), row_spec(D_MODEL), row_spec(D_MODEL), row_spec(LANES)]
    kv_block = (None, None, PAGE_SIZE, FOX_HEADS, FOX_HD)
    in_specs += [pl.BlockSpec(kv_block, page_map(g, 5)) for g in range(g_pages)]
    in_specs += [pl.BlockSpec(kv_block, page_map(g, 5)) for g in range(g_pages)]
    in_specs += [pl.BlockSpec((None, 2, HALF_ROWS // LANES, LANES), page_map(g, 4)) for g in range(g_pages)]
    rows = HALF_HEADS * t
    grid_spec = pltpu.PrefetchScalarGridSpec(
        num_scalar_prefetch=1,
        grid=(n_batch, steps),
        in_specs=in_specs,
        out_specs=pl.BlockSpec((t, D_MODEL), lambda b, j, pt: (b, 0)),
        scratch_shapes=[pltpu.VMEM((2, rows, FOX_HD), BF16),
                        pltpu.VMEM((2, rows, FOX_HD), F32),
                        pltpu.VMEM((2, rows, 1), F32),
                        pltpu.VMEM((2, rows, 1), F32),
                        pltpu.VMEM((2 * rows, 1), F32),
                        pltpu.VMEM((2, 1, LANES), F32)],
    )
    return pl.pallas_call(
        _fox_sample_kernel,
        grid_spec=grid_spec,
        out_shape=jax.ShapeDtypeStruct((n_batch * t, D_MODEL), F32),
        compiler_params=_params(("arbitrary", "arbitrary")),
        name="fox_sample",
    )(pt_flat, qs, ks, vs, lfs, *([cache_k] * g_pages), *([cache_v] * g_pages), *([lf_halves] * g_pages))


def _pad_cols(w, width):
    return jnp.pad(w, ((0, 0), (0, width - w.shape[1])))


def kernel(x_prompt, x_sample, state_pool, state_gla, cache_k, cache_v, cache_logf, page_table, meta_tokens,
           norm_mix_e, w_in_e, pool_w, pool_scale, gla_w_a2, gla_b_a, gla_norm, w_out_e, norm_mix_o, w_in_o,
           fox_f_bias, w_out_o, norm_mlp, w_up, w_down, norm_final):
    n_bp, seq_p = x_prompt.shape[0], x_prompt.shape[1] + N_META
    n_bs, seq_s = x_sample.shape[0], x_sample.shape[1]
    meta = jnp.broadcast_to(meta_tokens[None].astype(x_prompt.dtype), (n_bp, N_META, D_MODEL))
    hp = jnp.concatenate([meta, x_prompt], axis=1).reshape(n_bp * seq_p, D_MODEL)
    hs = x_sample.reshape(n_bs * seq_s, D_MODEL)
    row = lambda v: v.reshape(1, -1)

    w_e = w_in_e[0].astype(BF16)
    n_main = D_POOL + 2 * HK + 2 * HV
    (zp, zxp), (zs, zxs), (w_up0, w_out_e_bf) = _in_proj(
        hp, hs, row(norm_mix_e[0]), w_e, _pad_cols(w_e[:, n_main:], LANES), jnp.zeros((1, LANES), F32),
        sections=(n_main // 1024,), tm=ROW_BLOCK, tn=1024, logsig_extra=False, casts=((w_up, 0), (w_out_e, 0)))
    pw = pool_w[0].astype(BF16)
    ps = row(pool_scale[0])
    wa2 = jnp.pad(gla_w_a2[0], ((0, LANES - GLA_RANK), (0, 0))).astype(BF16)
    ba = row(gla_b_a[0])
    gn = row(gla_norm[0])
    pool_p, pool_buf_p = _pool_prompt(zp, pw, ps, n_bp)
    gla_p, gla_state_p, (w_down0,) = _gla_prompt(zp.reshape(n_bp, seq_p, n_main), zxp.reshape(n_bp, seq_p, LANES),
                                                 wa2, ba, gn, casts=((w_down, 0),))
    pool_s, gla_s, pool_buf_s, gla_state_s = _even_sample(zs, zxs, state_pool, state_gla, pw, ps, wa2, ba, gn,
                                                          n_bs, seq_s)
    hp, hs = _out_proj(hp, hs, w_out_e_bf, [pool_p, gla_p.reshape(n_bp * seq_p, HV)], [pool_s, gla_s])
    hp, hs, (w_up1, w_down1, w_out_o_bf) = _mlp(
        hp, hs, row(norm_mlp[0]), row(norm_final), w_up0, w_down0, final_norm=False,
        casts=((w_up, 1), (w_down, 1), (w_out_o, 0)))

    w_o = w_in_o[0].astype(BF16)
    fb = jnp.pad(row(fox_f_bias[0]), ((0, 0), (0, LANES - FOX_HEADS)))
    (qp, kp, vp, lfp), (qs, ks, vs, lfs), _ = _in_proj(
        hp, hs, row(norm_mix_o[0]), w_o, _pad_cols(w_o[:, 3 * D_MODEL:], LANES), fb,
        sections=(2, 2, 2), tm=SEQ_BLOCK, tn=1024, logsig_extra=True)
    shape3 = lambda a: a.reshape(n_bp, seq_p, a.shape[-1])
    augq, augk = _forget_cumsum(shape3(lfp))
    att_p = _fox_prompt(shape3(qp), shape3(kp), shape3(vp), augq, augk)
    att_s = _fox_sample(qs, ks, vs, lfs, cache_k, cache_v, cache_logf, page_table, n_bs, seq_s)
    hp, hs = _out_proj(hp, hs, w_out_o_bf, [att_p.reshape(n_bp * seq_p, D_MODEL)], [att_s])
    yp, ys, _ = _mlp(hp, hs, row(norm_mlp[1]), row(norm_final), w_up1, w_down1, final_norm=True)

    y_prompt = yp.reshape(n_bp, seq_p, D_MODEL)[:, N_META:]
    y_sample = ys.reshape(n_bs, seq_s, D_MODEL)
    heads = lambda a, b, s: a.reshape(1, b, s, FOX_HEADS, FOX_HD)
    lf_out = lambda a, b, s: a[:, :FOX_HEADS].reshape(1, b, s, FOX_HEADS)
    return (y_prompt, y_sample, pool_buf_p, pool_buf_s, gla_state_p, gla_state_s,
            heads(kp, n_bp, seq_p), heads(ks, n_bs, seq_s), heads(vp, n_bp, seq_p), heads(vs, n_bs, seq_s),
            lf_out(lfp, n_bp, seq_p), lf_out(lfs, n_bs, seq_s))
```

```python
import functools

import jax
import jax.numpy as jnp
from jax import lax
from jax.experimental import pallas as pl
from jax.experimental.pallas import tpu as pltpu

F32 = jnp.float32
BF16 = jnp.bfloat16
HIGHEST = lax.Precision.HIGHEST

D_MODEL = 2048
N_META = 16
EPS = 1e-6
POOL_WINDOWS = (2, 4, 8, 16)
D_POOL = 1024
POOL_GROUP = 256
POOL_BUF = 15
GLA_HEADS = 4
GLA_DK = 128
GLA_DV = 256
GLA_RANK = 16
GLA_NORMALIZER = 16.0
FOX_HEADS = 16
FOX_HD = 128
PAGE_SIZE = 128
D_FF = 8192
HK = GLA_HEADS * GLA_DK
HV = GLA_HEADS * GLA_DV

LANES = 128
NEG_BIG = -1e30
VMEM_LIMIT = 56 * 1024 * 1024

ROW_BLOCK = 1032
SEQ_BLOCK = 688
GLA_CHUNK = 48
PAGES_PER_STEP = 8

NT_DIMS = (((1,), (1,)), ((), ()))
TN_DIMS = (((0,), (0,)), ((), ()))


def _params(semantics):
    return pltpu.CompilerParams(dimension_semantics=semantics, vmem_limit_bytes=VMEM_LIMIT)


def _dot(a, b, **kw):
    return jnp.dot(a, b, preferred_element_type=F32, **kw)


def _dot_nt(a, b, **kw):
    return lax.dot_general(a, b, NT_DIMS, preferred_element_type=F32, **kw)


def _dot_tn(a, b, **kw):
    return lax.dot_general(a, b, TN_DIMS, preferred_element_type=F32, **kw)


def _log_sigmoid(x):
    return jnp.minimum(x, 0.0) - jnp.log1p(jnp.exp(-jnp.abs(x)))


def _rms_norm(x, g):
    ms = jnp.mean(x * x, axis=-1, keepdims=True)
    return x * lax.rsqrt(ms + EPS) * g


BF16_ROWS = 16


def _side_cast_specs(srcs, n_steps, step_index):
    in_specs, out_specs, out_shape = [], [], []
    for arr, layer in srcs:
        rows, cols = arr.shape[1], arr.shape[2]
        blk = pl.cdiv(pl.cdiv(rows, n_steps), BF16_ROWS) * BF16_ROWS
        last = pl.cdiv(rows, blk) - 1
        in_specs.append(pl.BlockSpec(
            (None, blk, cols), lambda *g, layer=layer, last=last: (layer, jnp.minimum(step_index(*g), last), 0)))
        out_specs.append(pl.BlockSpec((blk, cols), lambda *g, last=last: (jnp.minimum(step_index(*g), last), 0)))
        out_shape.append(jax.ShapeDtypeStruct((rows, cols), BF16))
    return in_specs, out_specs, out_shape


def _side_cast(src_refs, dst_refs):
    for src_ref, dst_ref in zip(src_refs, dst_refs):
        dst_ref[...] = src_ref[...].astype(BF16)


def _in_proj_kernel(sections, logsig_extra, n_cast, xp_ref, xs_ref, g_ref, w_ref, wx_ref, bx_ref, *refs):
    nsec = len(sections)
    cast_src, refs = refs[:n_cast], refs[n_cast:]
    outs_p, zxp_ref = refs[:nsec], refs[nsec]
    outs_s, zxs_ref = refs[nsec + 1:2 * nsec + 1], refs[2 * nsec + 1]
    cast_dst = refs[2 * nsec + 2:2 * nsec + 2 + n_cast]
    xnp_ref, xns_ref = refs[2 * nsec + 2 + n_cast:]
    i = pl.program_id(0)
    j = pl.program_id(1)
    _side_cast(cast_src, cast_dst)

    def extra(xn):
        e = _dot(xn, wx_ref[...]) + bx_ref[...]
        return _log_sigmoid(e) if logsig_extra else e

    @pl.when(j == 0)
    def _():
        xn = _rms_norm(xp_ref[...], g_ref[...]).astype(BF16)
        xnp_ref[...] = xn
        zxp_ref[...] = extra(xn)

    @pl.when((j == 0) & (i == 0))
    def _():
        xn = _rms_norm(xs_ref[...], g_ref[...]).astype(BF16)
        xns_ref[...] = xn
        zxs_ref[...] = extra(xn)

    def write(outs, x_ref):
        z = _dot(x_ref[...], w_ref[...])
        off = 0
        for o_ref, n in zip(outs, sections):
            if nsec == 1:
                o_ref[...] = z.astype(o_ref.dtype)
            else:
                @pl.when((j >= off) & (j < off + n))
                def _(o_ref=o_ref):
                    o_ref[...] = z.astype(o_ref.dtype)
            off += n

    write(outs_p, xnp_ref)

    @pl.when(i == 0)
    def _():
        write(outs_s, xns_ref)


def _in_proj(xp, xs, g, w, wx, bx, sections, tm, tn, logsig_extra, casts=()):
    n_p, n_s = xp.shape[0], xs.shape[0]
    ncol = sum(sections)
    grid = (n_p // tm, ncol)
    offs = [sum(sections[:k]) for k in range(len(sections))]

    def p_map(off, n):
        return lambda i, j: (i, jnp.clip(j - off, 0, n - 1))

    def s_map(off, n):
        return lambda i, j: (0, jnp.clip(jnp.where(i == 0, j, ncol - 1) - off, 0, n - 1))

    out_shape, out_specs = [], []
    for rows, tr, mk in ((n_p, tm, p_map), (n_s, n_s, s_map)):
        for off, n in zip(offs, sections):
            out_shape.append(jax.ShapeDtypeStruct((rows, n * tn), F32))
            out_specs.append(pl.BlockSpec((tr, tn), mk(off, n)))
        out_shape.append(jax.ShapeDtypeStruct((rows, LANES), F32))
        out_specs.append(pl.BlockSpec((tr, LANES), (lambda i, j: (i, 0)) if rows == n_p else (lambda i, j: (0, 0))))
    c_in, c_out, c_shape = _side_cast_specs(casts, grid[0] * grid[1], lambda i, j: i * ncol + j)
    outs = pl.pallas_call(
        functools.partial(_in_proj_kernel, tuple(sections), logsig_extra, len(casts)),
        grid=grid,
        in_specs=[
            pl.BlockSpec((tm, D_MODEL), lambda i, j: (i, 0)),
            pl.BlockSpec((n_s, D_MODEL), lambda i, j: (0, 0)),
            pl.BlockSpec((1, D_MODEL), lambda i, j: (0, 0)),
            pl.BlockSpec((D_MODEL, tn), lambda i, j: (0, j)),
            pl.BlockSpec((D_MODEL, LANES), lambda i, j: (0, 0)),
            pl.BlockSpec((1, LANES), lambda i, j: (0, 0)),
        ] + c_in,
        out_specs=out_specs + c_out,
        out_shape=out_shape + c_shape,
        scratch_shapes=[pltpu.VMEM((tm, D_MODEL), BF16), pltpu.VMEM((n_s, D_MODEL), BF16)],
        compiler_params=_params(("arbitrary", "arbitrary")),
        name="in_proj",
    )(xp, xs, g, w, wx, bx, *[arr for arr, _ in casts])
    k = len(sections) + 1
    return outs[:k], outs[k:2 * k], outs[2 * k:]


def _out_proj_kernel(widths, xp_ref, xs_ref, w_ref, *refs):
    na = len(widths)
    ap, a_s = refs[:na], refs[na:2 * na]
    op_ref, os_ref = refs[2 * na:]
    i = pl.program_id(0)

    def compute(x_ref, a_refs, o_ref):
        acc = x_ref[...]
        off = 0
        for a_ref, kw in zip(a_refs, widths):
            acc = acc + _dot(a_ref[...].astype(BF16), w_ref[off:off + kw, :])
            off += kw
        o_ref[...] = acc

    compute(xp_ref, ap, op_ref)

    @pl.when(i == 0)
    def _():
        compute(xs_ref, a_s, os_ref)


def _out_proj(xp, xs, w, a_p, a_s, tm=SEQ_BLOCK, tn=D_MODEL):
    n_p, n_s = xp.shape[0], xs.shape[0]
    widths = tuple(a.shape[1] for a in a_p)
    ncol = D_MODEL // tn
    grid = (n_p // tm, ncol)
    s_col = lambda i, j: (0, jnp.where(i == 0, j, ncol - 1))
    in_specs = [
        pl.BlockSpec((tm, tn), lambda i, j: (i, j)),
        pl.BlockSpec((n_s, tn), s_col),
        pl.BlockSpec((D_MODEL, tn), lambda i, j: (0, j)),
    ]
    in_specs += [pl.BlockSpec((tm, kw), lambda i, j: (i, 0)) for kw in widths]
    in_specs += [pl.BlockSpec((n_s, kw), lambda i, j: (0, 0)) for kw in widths]
    return pl.pallas_call(
        functools.partial(_out_proj_kernel, widths),
        grid=grid,
        in_specs=in_specs,
        out_specs=[pl.BlockSpec((tm, tn), lambda i, j: (i, j)), pl.BlockSpec((n_s, tn), s_col)],
        out_shape=[jax.ShapeDtypeStruct((n_p, D_MODEL), F32), jax.ShapeDtypeStruct((n_s, D_MODEL), F32)],
        compiler_params=_params(("arbitrary", "arbitrary")),
        name="out_proj",
    )(xp, xs, w, *a_p, *a_s)


def _mlp_kernel(final_norm, n_cast, xp_ref, xs_ref, g_ref, gf_ref, wu_ref, wd_ref, *refs):
    cast_src, (op_ref, os_ref) = refs[:n_cast], refs[n_cast:n_cast + 2]
    cast_dst = refs[n_cast + 2:2 * n_cast + 2]
    xnp_ref, xns_ref = refs[2 * n_cast + 2:]
    i = pl.program_id(0)
    j = pl.program_id(1)
    last = pl.num_programs(1) - 1
    _side_cast(cast_src, cast_dst)

    def step(x_ref, xn_ref, o_ref):
        @pl.when(j == 0)
        def _():
            x = x_ref[...]
            xn_ref[...] = _rms_norm(x, g_ref[...]).astype(BF16)
            o_ref[...] = x

        a = _dot(xn_ref[...], wu_ref[...])
        a = jnp.square(jnp.maximum(a, 0.0)).astype(BF16)
        o_ref[...] += _dot(a, wd_ref[...])
        if final_norm:
            @pl.when(j == last)
            def _():
                o_ref[...] = _rms_norm(o_ref[...], gf_ref[...])

    step(xp_ref, xnp_ref, op_ref)

    @pl.when(i == 0)
    def _():
        step(xs_ref, xns_ref, os_ref)


def _mlp(xp, xs, g, gf, wu, wd, final_norm, casts=(), drop_meta=None, tf=512):
    n_p, n_s = xp.shape[0], xs.shape[0]
    if drop_meta is None:
        tm, n_out = ROW_BLOCK, n_p
        x_spec = pl.BlockSpec((tm, D_MODEL), lambda i, j: (i, 0))
    else:
        seq, n_meta = drop_meta
        tm = (seq - n_meta) // 2
        n_out = n_p // seq * (seq - n_meta)
        x_spec = pl.BlockSpec((pl.Element(tm), pl.Element(D_MODEL)),
                              lambda i, j: (pl.multiple_of((i // 2) * seq + n_meta + (i % 2) * tm, 8), 0))
    grid = (n_out // tm, D_FF // tf)
    c_in, c_out, c_shape = _side_cast_specs(casts, grid[0] * grid[1], lambda i, j: i * grid[1] + j)
    outs = pl.pallas_call(
        functools.partial(_mlp_kernel, final_norm, len(casts)),
        grid=grid,
        in_specs=[
            x_spec,
            pl.BlockSpec((n_s, D_MODEL), lambda i, j: (0, 0)),
            pl.BlockSpec((1, D_MODEL), lambda i, j: (0, 0)),
            pl.BlockSpec((1, D_MODEL), lambda i, j: (0, 0)),
            pl.BlockSpec((D_MODEL, tf), lambda i, j: (0, j)),
            pl.BlockSpec((tf, D_MODEL), lambda i, j: (j, 0)),
        ] + c_in,
        out_specs=[pl.BlockSpec((tm, D_MODEL), lambda i, j: (i, 0)),
                   pl.BlockSpec((n_s, D_MODEL), lambda i, j: (0, 0))] + c_out,
        out_shape=[jax.ShapeDtypeStruct((n_out, D_MODEL), F32), jax.ShapeDtypeStruct((n_s, D_MODEL), F32)] + c_shape,
        scratch_shapes=[pltpu.VMEM((tm, D_MODEL), BF16), pltpu.VMEM((n_s, D_MODEL), BF16)],
        compiler_params=_params(("arbitrary", "arbitrary")),
        name="mlp",
    )(xp, xs, g, gf, wu, wd, *[arr for arr, _ in casts])
    return outs[0], outs[1], outs[2:]


def _pool_groups(ext_ref, rows, count_fn, pw_ref, ps_ref, o_ref):
    for g, w in enumerate(POOL_WINDOWS):
        cs = slice(g * POOL_GROUP, (g + 1) * POOL_GROUP)
        u = ext_ref[16:16 + rows, cs]
        acc = u
        for back in range(1, w):
            acc = acc + ext_ref[16 - back:16 - back + rows, cs]
        pooled = acc / count_fn(w) - u
        mixed = _dot(pooled.astype(BF16), pw_ref[g]) * ps_ref[:, cs]
        o_ref[:, cs] = mixed.astype(o_ref.dtype)


def _pool_prompt_kernel(z_ref, pw_ref, ps_ref, o_ref, buf_ref, ext_ref):
    blk = pl.program_id(0) % (2064 // SEQ_BLOCK)

    @pl.when(blk == 0)
    def _():
        ext_ref[0:16, :] = jnp.zeros((16, D_POOL), F32)

    ext_ref[16:16 + SEQ_BLOCK, :] = z_ref[...]
    pos = blk * SEQ_BLOCK + lax.broadcasted_iota(jnp.int32, (SEQ_BLOCK, 1), 0)
    _pool_groups(ext_ref, SEQ_BLOCK, lambda w: jnp.minimum(w, pos + 1).astype(F32), pw_ref, ps_ref, o_ref)

    @pl.when(blk == 2064 // SEQ_BLOCK - 1)
    def _():
        buf_ref[...] = ext_ref[16 + SEQ_BLOCK - POOL_BUF:16 + SEQ_BLOCK, :]

    ext_ref[0:16, :] = ext_ref[SEQ_BLOCK:SEQ_BLOCK + 16, :]


def _pool_prompt(z_main, pw, ps, n_batch):
    n_p = z_main.shape[0]
    per_seq = 2064 // SEQ_BLOCK
    return pl.pallas_call(
        _pool_prompt_kernel,
        grid=(n_p // SEQ_BLOCK,),
        in_specs=[
            pl.BlockSpec((SEQ_BLOCK, D_POOL), lambda r: (r, 0)),
            pl.BlockSpec((len(POOL_WINDOWS), POOL_GROUP, POOL_GROUP), lambda r: (0, 0, 0)),
            pl.BlockSpec((1, D_POOL), lambda r: (0, 0)),
        ],
        out_specs=[pl.BlockSpec((SEQ_BLOCK, D_POOL), lambda r: (r, 0)),
                   pl.BlockSpec((None, None, POOL_BUF, D_POOL), lambda r: (0, r // per_seq, 0, 0))],
        out_shape=[jax.ShapeDtypeStruct((n_p, D_POOL), BF16),
                   jax.ShapeDtypeStruct((1, n_batch, POOL_BUF, D_POOL), F32)],
        scratch_shapes=[pltpu.VMEM((16 + SEQ_BLOCK, D_POOL), F32)],
        compiler_params=_params(("arbitrary",)),
        name="pool_prompt",
    )(z_main, pw, ps)


def _gla_chunk(qk, v, r, alr, wa2, ba, gnorm, st_get, st_set, o_set):
    c = qk.shape[0]
    x = _dot(alr.astype(BF16), wa2) + ba
    la = _log_sigmoid(x) * (1.0 / GLA_NORMALIZER)
    ri = lax.broadcasted_iota(jnp.int32, (c, c), 0)
    ci = lax.broadcasted_iota(jnp.int32, (c, c), 1)
    causal = ri >= ci
    b = _dot(causal.astype(F32), la, precision=HIGHEST)
    mid = c // 2 - 1
    bm = b[mid:mid + 1, :]
    be = b[c - 1:c, :]
    q = qk[:, :HK] * (GLA_DK ** -0.5)
    k = qk[:, HK:]
    qt = q * jnp.exp(b - bm)
    kt = k * jnp.exp(bm - b)
    qi = qt * jnp.exp(bm)
    khat = kt * jnp.exp(be - bm)
    e_end = jnp.exp(be)
    for h in range(GLA_HEADS):
        ks = slice(h * GLA_DK, (h + 1) * GLA_DK)
        vs = slice(h * GLA_DV, (h + 1) * GLA_DV)
        vh = v[:, vs].astype(BF16)
        att = _dot_nt(qt[:, ks].astype(BF16), kt[:, ks].astype(BF16))
        att = jnp.where(causal, att, 0.0)
        st = st_get(h)
        o = _dot(att.astype(BF16), vh) + _dot_nt(qi[:, ks].astype(BF16), st.astype(BF16))
        st_set(h, st * e_end[:, ks] + _dot_tn(vh, khat[:, ks].astype(BF16)))
        on = _rms_norm(o, gnorm[:, vs])
        rh = r[:, vs]
        gate = rh / (1.0 + jnp.exp(-rh))
        o_set(h, on * gate)


def _gla_prompt_kernel(n_batch, n_cast, qk_ref, v_ref, r_ref, alr_ref, wa2_ref, ba_ref, gn_ref, *refs):
    cast_src, (o_ref, s_ref) = refs[:n_cast], refs[n_cast:n_cast + 2]
    cast_dst, st_ref = refs[n_cast + 2:2 * n_cast + 2], refs[2 * n_cast + 2]
    c = pl.program_id(0)
    _side_cast(cast_src, cast_dst)

    @pl.when(c == 0)
    def _():
        st_ref[...] = jnp.zeros(st_ref.shape, F32)

    for bi in range(n_batch):
        def st_get(h, bi=bi):
            return st_ref[bi * GLA_HEADS + h]

        def st_set(h, val, bi=bi):
            st_ref[bi * GLA_HEADS + h] = val

        def o_set(h, val, bi=bi):
            o_ref[bi, :, h * GLA_DV:(h + 1) * GLA_DV] = val.astype(o_ref.dtype)

        _gla_chunk(qk_ref[bi], v_ref[bi], r_ref[bi], alr_ref[bi], wa2_ref[...], ba_ref[...], gn_ref[...],
                   st_get, st_set, o_set)

    @pl.when(c == pl.num_programs(0) - 1)
    def _():
        for bi in range(n_batch):
            for h in range(GLA_HEADS):
                s_ref[0, bi, h] = st_ref[bi * GLA_HEADS + h].T


def _gla_prompt(z_main3, zx3, wa2, ba, gn, casts=()):
    n_batch, seq = z_main3.shape[0], z_main3.shape[1]
    c_in, c_out, c_shape = _side_cast_specs(casts, seq // GLA_CHUNK, lambda c: c)
    blk = lambda col: pl.BlockSpec((n_batch, GLA_CHUNK, 1024), lambda c, col=col: (0, c, col))
    const2 = lambda shape: pl.BlockSpec(shape, lambda c: (0, 0))
    outs = pl.pallas_call(
        functools.partial(_gla_prompt_kernel, n_batch, len(casts)),
        grid=(seq // GLA_CHUNK,),
        in_specs=[blk(1), blk(2), blk(3),
                  pl.BlockSpec((n_batch, GLA_CHUNK, LANES), lambda c: (0, c, 0)),
                  const2((LANES, HK)), const2((1, HK)), const2((1, HV))] + c_in,
        out_specs=[pl.BlockSpec((n_batch, GLA_CHUNK, HV), lambda c: (0, c, 0)),
                   pl.BlockSpec((1, n_batch, GLA_HEADS, GLA_DK, GLA_DV), lambda c: (0, 0, 0, 0, 0))] + c_out,
        out_shape=[jax.ShapeDtypeStruct((n_batch, seq, HV), BF16),
                   jax.ShapeDtypeStruct((1, n_batch, GLA_HEADS, GLA_DK, GLA_DV), F32)] + c_shape,
        scratch_shapes=[pltpu.VMEM((n_batch * GLA_HEADS, GLA_DV, GLA_DK), F32)],
        compiler_params=_params(("arbitrary",)),
        name="gla_prompt",
    )(z_main3, z_main3, z_main3, zx3, wa2, ba, gn, *[arr for arr, _ in casts])
    return outs[0], outs[1], outs[2:]


def _even_sample_kernel(u_ref, qk_ref, v_ref, r_ref, alr_ref, pbuf_ref, s0_ref, pw_ref, ps_ref, wa2_ref, ba_ref,
                        gn_ref, po_ref, go_ref, nbuf_ref, ns_ref, ext_ref):
    t = u_ref.shape[0]
    ext_ref[0:1, :] = jnp.zeros((1, D_POOL), F32)
    ext_ref[1:16, :] = pbuf_ref[...]
    ext_ref[16:16 + t, :] = u_ref[...]
    _pool_groups(ext_ref, t, lambda w: float(w), pw_ref, ps_ref, po_ref)
    nbuf_ref[...] = ext_ref[16 + t - POOL_BUF:16 + t, :]

    def st_get(h):
        return s0_ref[h].T

    def st_set(h, val):
        ns_ref[h] = val.T

    def o_set(h, val):
        go_ref[:, h * GLA_DV:(h + 1) * GLA_DV] = val

    _gla_chunk(qk_ref[...], v_ref[...], r_ref[...], alr_ref[...], wa2_ref[...], ba_ref[...], gn_ref[...],
               st_get, st_set, o_set)


def _even_sample(zs_main, zxs, pool_buf, gla_s0, pw, ps, wa2, ba, gn, n_batch, t):
    blk = lambda col: pl.BlockSpec((t, 1024), lambda b, col=col: (b, col))
    const = lambda shape: pl.BlockSpec(shape, lambda b: (0,) * len(shape))
    return pl.pallas_call(
        _even_sample_kernel,
        grid=(n_batch,),
        in_specs=[blk(0), blk(1), blk(2), blk(3),
                  pl.BlockSpec((t, LANES), lambda b: (b, 0)),
                  pl.BlockSpec((None, None, POOL_BUF, D_POOL), lambda b: (0, b, 0, 0)),
                  pl.BlockSpec((None, None, GLA_HEADS, GLA_DK, GLA_DV), lambda b: (0, b, 0, 0, 0)),
                  const((len(POOL_WINDOWS), POOL_GROUP, POOL_GROUP)), const((1, D_POOL)),
                  const((LANES, HK)), const((1, HK)), const((1, HV))],
        out_specs=[pl.BlockSpec((t, D_POOL), lambda b: (b, 0)),
                   pl.BlockSpec((t, HV), lambda b: (b, 0)),
                   pl.BlockSpec((None, None, POOL_BUF, D_POOL), lambda b: (0, b, 0, 0)),
                   pl.BlockSpec((None, None, GLA_HEADS, GLA_DK, GLA_DV), lambda b: (0, b, 0, 0, 0))],
        out_shape=[jax.ShapeDtypeStruct((n_batch * t, D_POOL), F32),
                   jax.ShapeDtypeStruct((n_batch * t, HV), F32),
                   jax.ShapeDtypeStruct((1, n_batch, POOL_BUF, D_POOL), F32),
                   jax.ShapeDtypeStruct((1, n_batch, GLA_HEADS, GLA_DK, GLA_DV), F32)],
        scratch_shapes=[pltpu.VMEM((16 + t, D_POOL), F32)],
        compiler_params=_params(("arbitrary",)),
        name="even_sample",
    )(zs_main, zs_main, zs_main, zs_main, zxs, pool_buf, gla_s0, pw, ps, wa2, ba, gn)


HEADS_PER_STEP = 8
AUG_LANES = 8

def _split3(x):
    hi = x.astype(BF16)
    r1 = x - hi.astype(F32)
    mid = r1.astype(BF16)
    lo = (r1 - mid.astype(F32)).astype(BF16)
    return hi, mid, lo


def _forget_cumsum_kernel(lf_ref, augq_ref, augk_ref, a_ref, b_ref):
    seq = lf_ref.shape[0]
    pad = a_ref.shape[0] - seq
    a_ref[0:pad, :] = jnp.zeros((pad, LANES), F32)
    b_ref[0:pad, :] = jnp.zeros((pad, LANES), F32)
    a_ref[pad:, :] = lf_ref[...]
    src, dst = a_ref, b_ref
    shift = 1
    while shift < seq:
        dst[pad:, :] = src[pad:, :] + src[pad - shift:pad - shift + seq, :]
        src, dst = dst, src
        shift *= 2
    parts = _split3(src[pad:, :] * (FOX_HD ** 0.5))
    head = lax.broadcasted_iota(jnp.int32, (LANES, LANES), 0)
    dest = lax.broadcasted_iota(jnp.int32, (LANES, LANES), 1)
    lane = lax.broadcasted_iota(jnp.int32, (seq, LANES), 1)
    used = lane < HEADS_PER_STEP * AUG_LANES
    ones_q = (used & (lane % AUG_LANES >= 3) & (lane % AUG_LANES < 6)).astype(F32)
    ones_k = (used & (lane % AUG_LANES < 3)).astype(F32)
    for hg in range(FOX_HEADS // HEADS_PER_STEP):
        from_head = (head == hg * HEADS_PER_STEP + dest // AUG_LANES) & (dest < HEADS_PER_STEP * AUG_LANES)
        ft = sum(_dot(parts[c], (from_head & (dest % AUG_LANES == c)).astype(BF16)) for c in range(3))
        fs = sum(_dot(parts[c], (from_head & (dest % AUG_LANES == 3 + c)).astype(BF16)) for c in range(3))
        augq_ref[hg] = (ft + ones_q).astype(BF16)
        augk_ref[hg] = (ones_k - fs).astype(BF16)


def _forget_cumsum(lf3):
    n_batch, seq = lf3.shape[0], lf3.shape[1]
    pad = 2048
    n_groups = FOX_HEADS // HEADS_PER_STEP
    aug_spec = pl.BlockSpec((None, n_groups, seq, LANES), lambda b: (b, 0, 0, 0))
    aug_shape = jax.ShapeDtypeStruct((n_batch, n_groups, seq, LANES), BF16)
    return pl.pallas_call(
        _forget_cumsum_kernel,
        grid=(n_batch,),
        in_specs=[pl.BlockSpec((None, seq, LANES), lambda b: (b, 0, 0))],
        out_specs=[aug_spec, aug_spec],
        out_shape=[aug_shape, aug_shape],
        scratch_shapes=[pltpu.VMEM((pad + seq, LANES), F32), pltpu.VMEM((pad + seq, LANES), F32)],
        compiler_params=_params(("arbitrary",)),
        name="forget_cumsum",
    )(lf3)


def _fox_prompt_kernel(q_ref, k_ref, v_ref, augq_ref, augk_ref, o_ref, m_ref, acc_ref, qa_ref):
    qi = pl.program_id(2)
    ki = pl.program_id(3)
    tq = q_ref.shape[0]
    to_log2 = (FOX_HD ** -0.5) * 1.4426950408889634

    @pl.when(ki == 0)
    def _():
        m_ref[...] = jnp.full(m_ref.shape, NEG_BIG, F32)
        acc_ref[...] = jnp.zeros(acc_ref.shape, F32)
        lane = lax.broadcasted_iota(jnp.int32, (tq, LANES), 1)
        augq = augq_ref[...]
        for hh in range(HEADS_PER_STEP):
            cs = slice(hh * FOX_HD, (hh + 1) * FOX_HD)
            own = jnp.where(lane // AUG_LANES == hh, augq, jnp.zeros_like(augq))
            qa_ref[hh] = jnp.concatenate([q_ref[:, cs].astype(BF16), own], axis=1)

    def block(diagonal):
        augk = augk_ref[...]
        ones_col = (lax.broadcasted_iota(jnp.int32, (tq, LANES), 1) == 0).astype(BF16)
        if diagonal:
            visible = (lax.broadcasted_iota(jnp.int32, (tq, tq), 1) <= lax.broadcasted_iota(jnp.int32, (tq, tq), 0))

        for hh in range(HEADS_PER_STEP):
            cs = slice(hh * FOX_HD, (hh + 1) * FOX_HD)
            ka = jnp.concatenate([k_ref[:, cs].astype(BF16), augk], axis=1)
            s = _dot_nt(qa_ref[hh], ka) * to_log2
            if diagonal:
                s = jnp.where(visible, s, NEG_BIG)
            va = jnp.concatenate([v_ref[:, cs].astype(BF16), ones_col], axis=1)
            m_old = m_ref[hh]
            m_new = jnp.maximum(m_old, jnp.max(s, axis=-1, keepdims=True))
            p = jnp.exp2(s - m_new)
            acc_ref[hh] = jnp.exp2(m_old - m_new) * acc_ref[hh] + _dot(p.astype(BF16), va)
            m_ref[hh] = m_new

    @pl.when(ki < qi)
    def _():
        block(diagonal=False)

    @pl.when(ki == qi)
    def _():
        block(diagonal=True)
        for hh in range(HEADS_PER_STEP):
            acc = acc_ref[hh]
            o_ref[:, hh * FOX_HD:(hh + 1) * FOX_HD] = (acc[:, :FOX_HD] / acc[:, FOX_HD:FOX_HD + 1]).astype(o_ref.dtype)


def _fox_prompt(q3, k3, v3, augq, augk):
    n_batch, seq = q3.shape[0], q3.shape[1]
    nblk = seq // SEQ_BLOCK
    wcol = HEADS_PER_STEP * FOX_HD
    kv_spec = pl.BlockSpec((None, SEQ_BLOCK, wcol), lambda b, hg, qi, ki: (b, jnp.minimum(ki, qi), hg))
    return pl.pallas_call(
        _fox_prompt_kernel,
        grid=(n_batch, FOX_HEADS // HEADS_PER_STEP, nblk, nblk),
        in_specs=[
            pl.BlockSpec((None, SEQ_BLOCK, wcol), lambda b, hg, qi, ki: (b, qi, hg)),
            kv_spec, kv_spec,
            pl.BlockSpec((None, None, SEQ_BLOCK, LANES), lambda b, hg, qi, ki: (b, hg, qi, 0)),
            pl.BlockSpec((None, None, SEQ_BLOCK, LANES), lambda b, hg, qi, ki: (b, hg, jnp.minimum(ki, qi), 0)),
        ],
        out_specs=pl.BlockSpec((None, SEQ_BLOCK, wcol), lambda b, hg, qi, ki: (b, qi, hg)),
        out_shape=jax.ShapeDtypeStruct((n_batch, seq, D_MODEL), BF16),
        scratch_shapes=[pltpu.VMEM((HEADS_PER_STEP, SEQ_BLOCK, 1), F32),
                        pltpu.VMEM((HEADS_PER_STEP, SEQ_BLOCK, 2 * FOX_HD), F32),
                        pltpu.VMEM((HEADS_PER_STEP, SEQ_BLOCK, 2 * FOX_HD), BF16)],
        compiler_params=_params(("arbitrary", "arbitrary", "arbitrary", "arbitrary")),
        name="fox_prompt",
    )(q3, k3, v3, augq, augk)


HALF_HEADS = FOX_HEADS // 2
HALF_ROWS = PAGE_SIZE * HALF_HEADS


def _later_log_forget(x, carry):
    lane = lax.broadcasted_iota(jnp.int32, x.shape, 1)
    sub = lax.broadcasted_iota(jnp.int32, x.shape, 0)
    after = x
    before = x
    shift = HALF_HEADS
    while shift < LANES:
        after = after + jnp.where(lane + shift < LANES, pltpu.roll(after, LANES - shift, axis=1), 0.0)
        before = before + jnp.where(lane >= shift, pltpu.roll(before, shift, axis=1), 0.0)
        shift *= 2
    row_total = after + before - x
    below = row_total
    shift = 1
    while shift < x.shape[0]:
        below = below + jnp.where(sub + shift < x.shape[0], pltpu.roll(below, x.shape[0] - shift, axis=0), 0.0)
        shift *= 2
    later = (after - x) + (below - row_total) + carry
    return later, carry + below[0:1, :]


def _fox_sample_kernel(pt_ref, qs_ref, kn_ref, vn_ref, lfn_ref, *refs):
    g_pages = PAGES_PER_STEP
    k_pages = refs[:g_pages]
    v_pages = refs[g_pages:2 * g_pages]
    lf_pages = refs[2 * g_pages:3 * g_pages]
    o_ref = refs[3 * g_pages]
    q2_ref, acc_ref, m_ref, l_ref, lcol_ref, carry_ref = refs[3 * g_pages + 1:]
    j = pl.program_id(1)
    t = qs_ref.shape[0]
    rows = HALF_HEADS * t
    scale = FOX_HD ** -0.5
    row = lax.broadcasted_iota(jnp.int32, (rows, LANES), 0)
    lane = lax.broadcasted_iota(jnp.int32, (rows, LANES), 1)

    def online_update(hf, s, v):
        m_old = m_ref[hf]
        m_new = jnp.maximum(m_old, jnp.max(s, axis=-1, keepdims=True))
        alpha = jnp.exp(m_old - m_new)
        p = jnp.exp(s - m_new)
        l_ref[hf] = alpha * l_ref[hf] + jnp.sum(p, axis=-1, keepdims=True)
        acc_ref[hf] = alpha * acc_ref[hf] + _dot(p.astype(BF16), v)
        m_ref[hf] = m_new

    @pl.when(j == 0)
    def _():
        m_ref[...] = jnp.full(m_ref.shape, NEG_BIG, F32)
        l_ref[...] = jnp.zeros(l_ref.shape, F32)
        acc_ref[...] = jnp.zeros(acc_ref.shape, F32)
        carry_ref[...] = jnp.zeros(carry_ref.shape, F32)
        full_row = lax.broadcasted_iota(jnp.int32, (LANES, LANES), 0)
        full_lane = lax.broadcasted_iota(jnp.int32, (LANES, LANES), 1)
        lfn = jnp.where(lax.broadcasted_iota(jnp.int32, (t, LANES), 1) < FOX_HEADS, lfn_ref[...], 0.0)
        tri = (lax.broadcasted_iota(jnp.int32, (t, t), 0) >= lax.broadcasted_iota(jnp.int32, (t, t), 1))
        l_new = _dot(tri.astype(F32), lfn, precision=HIGHEST)
        l_pad = jnp.concatenate([l_new, jnp.zeros((LANES - t, LANES), F32)], axis=0)
        head_sel = (full_row // t == full_lane).astype(F32)
        l_t = _dot_nt(head_sel, l_pad, precision=HIGHEST)
        lcol = jnp.sum(jnp.where(full_lane == full_row % t, l_t, 0.0), axis=-1, keepdims=True)
        lrow = jnp.sum(jnp.where(full_lane == full_row, lcol, 0.0), axis=0, keepdims=True)
        lcol_ref[...] = lcol
        heads = range(FOX_HEADS)
        k_new = jnp.concatenate([kn_ref[:, h * FOX_HD:(h + 1) * FOX_HD] for h in heads], axis=0).astype(BF16)
        v_new = jnp.concatenate([vn_ref[:, h * FOX_HD:(h + 1) * FOX_HD] for h in heads], axis=0).astype(BF16)
        for hf in range(2):
            hs = range(hf * HALF_HEADS, (hf + 1) * HALF_HEADS)
            q2 = jnp.concatenate([qs_ref[:, h * FOX_HD:(h + 1) * FOX_HD] for h in hs], axis=0).astype(BF16)
            q2_ref[hf] = q2
            s = _dot_nt(q2, k_new) * scale + lcol[hf * rows:(hf + 1) * rows] - lrow
            visible = (lane // t == hf * HALF_HEADS + row // t) & (lane % t <= row % t)
            online_update(hf, jnp.where(visible, s, NEG_BIG), v_new)

    @pl.when(j > 0)
    def _():
        other_head = jnp.where(lane % HALF_HEADS == row // t, 0.0, NEG_BIG)
        for hf in range(2):
            hsl = slice(hf * HALF_HEADS, (hf + 1) * HALF_HEADS)
            carry = carry_ref[hf]
            bias, k_parts, v_parts = [], [], []
            for g in range(g_pages):
                k_half = k_pages[g][:, hsl, :].reshape(HALF_ROWS, FOX_HD).astype(BF16)
                v_half = v_pages[g][:, hsl, :].reshape(HALF_ROWS, FOX_HD).astype(BF16)
                k_parts.append(k_half)
                v_parts.append(v_half)
                later, carry = _later_log_forget(lf_pages[g][hf], carry)
                bias += [later[a:a + 1, :] + other_head for a in range(later.shape[0])]
            carry_ref[hf] = carry
            s = _dot_nt(q2_ref[hf], jnp.concatenate(k_parts, axis=0)) * scale + lcol_ref[hf * rows:(hf + 1) * rows, :]
            online_update(hf, s + jnp.concatenate(bias, axis=1), jnp.concatenate(v_parts, axis=0))

    @pl.when(j == pl.num_programs(1) - 1)
    def _():
        for hf in range(2):
            out = acc_ref[hf] / l_ref[hf]
            for hh in range(HALF_HEADS):
                h = hf * HALF_HEADS + hh
                o_ref[:, h * FOX_HD:(h + 1) * FOX_HD] = out[hh * t:(hh + 1) * t, :]


def _fox_sample(qs, ks, vs, lfs, cache_k, cache_v, cache_logf, page_table, n_batch, t):
    n_pages = page_table.shape[1]
    n_pool = cache_k.shape[1]
    g_pages = PAGES_PER_STEP
    steps = 1 + n_pages // g_pages
    pt_flat = page_table.reshape(-1)
    lf_halves = cache_logf[0].reshape(n_pool, PAGE_SIZE, 2, HALF_HEADS).transpose(0, 2, 1, 3)
    lf_halves = lf_halves.reshape(n_pool, 2, HALF_ROWS // LANES, LANES)

    def page_map(g, rank):
        def index_map(b, j, pt):
            idx = n_pages - 1 - (jnp.maximum(j, 1) - 1) * g_pages - g
            return (0,) * (rank - 4) + (pt[b * n_pages + idx], 0, 0, 0)
        return index_map

    row_spec = lambda w: pl.BlockSpec((t, w), lambda b, j, pt: (b, 0))
    in_specs = [row_spec(D_MODEL), row_spec(D_MODEL), row_spec(D_MODEL), row_spec(LANES)]
    kv_block = (None, None, PAGE_SIZE, FOX_HEADS, FOX_HD)
    in_specs += [pl.BlockSpec(kv_block, page_map(g, 5)) for g in range(g_pages)]
    in_specs += [pl.BlockSpec(kv_block, page_map(g, 5)) for g in range(g_pages)]
    in_specs += [pl.BlockSpec((None, 2, HALF_ROWS // LANES, LANES), page_map(g, 4)) for g in range(g_pages)]
    rows = HALF_HEADS * t
    grid_spec = pltpu.PrefetchScalarGridSpec(
        num_scalar_prefetch=1,
        grid=(n_batch, steps),
        in_specs=in_specs,
        out_specs=pl.BlockSpec((t, D_MODEL), lambda b, j, pt: (b, 0)),
        scratch_shapes=[pltpu.VMEM((2, rows, FOX_HD), BF16),
                        pltpu.VMEM((2, rows, FOX_HD), F32),
                        pltpu.VMEM((2, rows, 1), F32),
                        pltpu.VMEM((2, rows, 1), F32),
                        pltpu.VMEM((2 * rows, 1), F32),
                        pltpu.VMEM((2, 1, LANES), F32)],
    )
    return pl.pallas_call(
        _fox_sample_kernel,
        grid_spec=grid_spec,
        out_shape=jax.ShapeDtypeStruct((n_batch * t, D_MODEL), F32),
        compiler_params=_params(("arbitrary", "arbitrary")),
        name="fox_sample",
    )(pt_flat, qs, ks, vs, lfs, *([cache_k] * g_pages), *([cache_v] * g_pages), *([lf_halves] * g_pages))


def _pad_cols(w, width):
    return jnp.pad(w, ((0, 0), (0, width - w.shape[1])))


def kernel(x_prompt, x_sample, state_pool, state_gla, cache_k, cache_v, cache_logf, page_table, meta_tokens,
           norm_mix_e, w_in_e, pool_w, pool_scale, gla_w_a2, gla_b_a, gla_norm, w_out_e, norm_mix_o, w_in_o,
           fox_f_bias, w_out_o, norm_mlp, w_up, w_down, norm_final):
    n_bp, seq_p = x_prompt.shape[0], x_prompt.shape[1] + N_META
    n_bs, seq_s = x_sample.shape[0], x_sample.shape[1]
    meta = jnp.broadcast_to(meta_tokens[None].astype(x_prompt.dtype), (n_bp, N_META, D_MODEL))
    hp = jnp.concatenate([meta, x_prompt], axis=1).reshape(n_bp * seq_p, D_MODEL)
    hs = x_sample.reshape(n_bs * seq_s, D_MODEL)
    row = lambda v: v.reshape(1, -1)

    w_e = w_in_e[0].astype(BF16)
    n_main = D_POOL + 2 * HK + 2 * HV
    (zp, zxp), (zs, zxs), (w_out_e_bf,) = _in_proj(
        hp, hs, row(norm_mix_e[0]), w_e, _pad_cols(w_e[:, n_main:], LANES), jnp.zeros((1, LANES), F32),
        sections=(n_main // 1024,), tm=ROW_BLOCK, tn=1024, logsig_extra=False, casts=((w_out_e, 0),))
    pw = pool_w[0].astype(BF16)
    ps = row(pool_scale[0])
    wa2 = jnp.pad(gla_w_a2[0], ((0, LANES - GLA_RANK), (0, 0))).astype(BF16)
    ba = row(gla_b_a[0])
    gn = row(gla_norm[0])
    pool_p, pool_buf_p = _pool_prompt(zp, pw, ps, n_bp)
    gla_p, gla_state_p, (w_up0, w_down0) = _gla_prompt(
        zp.reshape(n_bp, seq_p, n_main), zxp.reshape(n_bp, seq_p, LANES), wa2, ba, gn, casts=((w_up, 0), (w_down, 0)))
    pool_s, gla_s, pool_buf_s, gla_state_s = _even_sample(zs, zxs, state_pool, state_gla, pw, ps, wa2, ba, gn,
                                                          n_bs, seq_s)
    hp, hs = _out_proj(hp, hs, w_out_e_bf, [pool_p, gla_p.reshape(n_bp * seq_p, HV)], [pool_s, gla_s])
    hp, hs, (w_up1, w_down1, w_out_o_bf) = _mlp(
        hp, hs, row(norm_mlp[0]), row(norm_final), w_up0, w_down0, final_norm=False,
        casts=((w_up, 1), (w_down, 1), (w_out_o, 0)))

    w_o = w_in_o[0].astype(BF16)
    fb = jnp.pad(row(fox_f_bias[0]), ((0, 0), (0, LANES - FOX_HEADS)))
    (qp, kp, vp, lfp), (qs, ks, vs, lfs), _ = _in_proj(
        hp, hs, row(norm_mix_o[0]), w_o, _pad_cols(w_o[:, 3 * D_MODEL:], LANES), fb,
        sections=(2, 2, 2), tm=SEQ_BLOCK, tn=1024, logsig_extra=True)
    shape3 = lambda a: a.reshape(n_bp, seq_p, a.shape[-1])
    augq, augk = _forget_cumsum(shape3(lfp))
    att_p = _fox_prompt(shape3(qp), shape3(kp), shape3(vp), augq, augk)
    att_s = _fox_sample(qs, ks, vs, lfs, cache_k, cache_v, cache_logf, page_table, n_bs, seq_s)
    hp, hs = _out_proj(hp, hs, w_out_o_bf, [att_p.reshape(n_bp * seq_p, D_MODEL)], [att_s])
    yp, ys, _ = _mlp(hp, hs, row(norm_mlp[1]), row(norm_final), w_up1, w_down1, final_norm=True,
                     drop_meta=(seq_p, N_META))

    y_prompt = yp.reshape(n_bp, seq_p - N_META, D_MODEL)
    y_sample = ys.reshape(n_bs, seq_s, D_MODEL)
    heads = lambda a, b, s: a.reshape(1, b, s, FOX_HEADS, FOX_HD)
    lf_out = lambda a, b, s: a[:, :FOX_HEADS].reshape(1, b, s, FOX_HEADS)
    return (y_prompt, y_sample, pool_buf_p, pool_buf_s, gla_state_p, gla_state_s,
            heads(kp, n_bp, seq_p), heads(ks, n_bs, seq_s), heads(vp, n_bp, seq_p), heads(vs, n_bs, seq_s),
            lf_out(lfp, n_bp, seq_p), lf_out(lfs, n_bs, seq_s))
```

```python
import functools

import jax
import jax.numpy as jnp
from jax import lax
from jax.experimental import pallas as pl
from jax.experimental.pallas import tpu as pltpu

F32 = jnp.float32
BF16 = jnp.bfloat16
HIGHEST = lax.Precision.HIGHEST

D_MODEL = 2048
N_META = 16
EPS = 1e-6
POOL_WINDOWS = (2, 4, 8, 16)
D_POOL = 1024
POOL_GROUP = 256
POOL_BUF = 15
GLA_HEADS = 4
GLA_DK = 128
GLA_DV = 256
GLA_RANK = 16
GLA_NORMALIZER = 16.0
FOX_HEADS = 16
FOX_HD = 128
PAGE_SIZE = 128
D_FF = 8192
HK = GLA_HEADS * GLA_DK
HV = GLA_HEADS * GLA_DV

LANES = 128
NEG_BIG = -1e30
VMEM_LIMIT = 56 * 1024 * 1024

ROW_BLOCK = 1032
SEQ_BLOCK = 688
GLA_CHUNK = 48
PAGES_PER_STEP = 8

NT_DIMS = (((1,), (1,)), ((), ()))
TN_DIMS = (((0,), (0,)), ((), ()))


def _params(semantics):
    return pltpu.CompilerParams(dimension_semantics=semantics, vmem_limit_bytes=VMEM_LIMIT)


def _dot(a, b, **kw):
    return jnp.dot(a, b, preferred_element_type=F32, **kw)


def _dot_nt(a, b, **kw):
    return lax.dot_general(a, b, NT_DIMS, preferred_element_type=F32, **kw)


def _dot_tn(a, b, **kw):
    return lax.dot_general(a, b, TN_DIMS, preferred_element_type=F32, **kw)


def _log_sigmoid(x):
    return jnp.minimum(x, 0.0) - jnp.log1p(jnp.exp(-jnp.abs(x)))


def _rms_norm(x, g):
    ms = jnp.mean(x * x, axis=-1, keepdims=True)
    return x * lax.rsqrt(ms + EPS) * g


BF16_ROWS = 16


def _side_cast_specs(srcs, n_steps, step_index):
    in_specs, out_specs, out_shape = [], [], []
    for arr, layer in srcs:
        rows, cols = arr.shape[1], arr.shape[2]
        blk = pl.cdiv(pl.cdiv(rows, n_steps), BF16_ROWS) * BF16_ROWS
        last = pl.cdiv(rows, blk) - 1
        in_specs.append(pl.BlockSpec(
            (None, blk, cols), lambda *g, layer=layer, last=last: (layer, jnp.minimum(step_index(*g), last), 0)))
        out_specs.append(pl.BlockSpec((blk, cols), lambda *g, last=last: (jnp.minimum(step_index(*g), last), 0)))
        out_shape.append(jax.ShapeDtypeStruct((rows, cols), BF16))
    return in_specs, out_specs, out_shape


def _side_cast(src_refs, dst_refs):
    for src_ref, dst_ref in zip(src_refs, dst_refs):
        dst_ref[...] = src_ref[...].astype(BF16)


def _in_proj_kernel(sections, logsig_extra, n_cast, xp_ref, xs_ref, g_ref, w_ref, wx_ref, bx_ref, *refs):
    nsec = len(sections)
    cast_src, refs = refs[:n_cast], refs[n_cast:]
    outs_p, zxp_ref = refs[:nsec], refs[nsec]
    outs_s, zxs_ref = refs[nsec + 1:2 * nsec + 1], refs[2 * nsec + 1]
    cast_dst = refs[2 * nsec + 2:2 * nsec + 2 + n_cast]
    xnp_ref, xns_ref = refs[2 * nsec + 2 + n_cast:]
    i = pl.program_id(0)
    j = pl.program_id(1)
    _side_cast(cast_src, cast_dst)

    def extra(xn):
        e = _dot(xn, wx_ref[...]) + bx_ref[...]
        return _log_sigmoid(e) if logsig_extra else e

    @pl.when(j == 0)
    def _():
        xn = _rms_norm(xp_ref[...], g_ref[...]).astype(BF16)
        xnp_ref[...] = xn
        zxp_ref[...] = extra(xn)

    @pl.when((j == 0) & (i == 0))
    def _():
        xn = _rms_norm(xs_ref[...], g_ref[...]).astype(BF16)
        xns_ref[...] = xn
        zxs_ref[...] = extra(xn)

    def write(outs, x_ref):
        z = _dot(x_ref[...], w_ref[...])
        off = 0
        for o_ref, n in zip(outs, sections):
            if nsec == 1:
                o_ref[...] = z.astype(o_ref.dtype)
            else:
                @pl.when((j >= off) & (j < off + n))
                def _(o_ref=o_ref):
                    o_ref[...] = z.astype(o_ref.dtype)
            off += n

    write(outs_p, xnp_ref)

    @pl.when(i == 0)
    def _():
        write(outs_s, xns_ref)


def _in_proj(xp, xs, g, w, wx, bx, sections, tm, tn, logsig_extra, casts=()):
    n_p, n_s = xp.shape[0], xs.shape[0]
    ncol = sum(sections)
    grid = (n_p // tm, ncol)
    offs = [sum(sections[:k]) for k in range(len(sections))]

    def p_map(off, n):
        return lambda i, j: (i, jnp.clip(j - off, 0, n - 1))

    def s_map(off, n):
        return lambda i, j: (0, jnp.clip(jnp.where(i == 0, j, ncol - 1) - off, 0, n - 1))

    out_shape, out_specs = [], []
    for rows, tr, mk in ((n_p, tm, p_map), (n_s, n_s, s_map)):
        for off, n in zip(offs, sections):
            out_shape.append(jax.ShapeDtypeStruct((rows, n * tn), F32))
            out_specs.append(pl.BlockSpec((tr, tn), mk(off, n)))
        out_shape.append(jax.ShapeDtypeStruct((rows, LANES), F32))
        out_specs.append(pl.BlockSpec((tr, LANES), (lambda i, j: (i, 0)) if rows == n_p else (lambda i, j: (0, 0))))
    c_in, c_out, c_shape = _side_cast_specs(casts, grid[0] * grid[1], lambda i, j: i * ncol + j)
    outs = pl.pallas_call(
        functools.partial(_in_proj_kernel, tuple(sections), logsig_extra, len(casts)),
        grid=grid,
        in_specs=[
            pl.BlockSpec((tm, D_MODEL), lambda i, j: (i, 0)),
            pl.BlockSpec((n_s, D_MODEL), lambda i, j: (0, 0)),
            pl.BlockSpec((1, D_MODEL), lambda i, j: (0, 0)),
            pl.BlockSpec((D_MODEL, tn), lambda i, j: (0, j)),
            pl.BlockSpec((D_MODEL, LANES), lambda i, j: (0, 0)),
            pl.BlockSpec((1, LANES), lambda i, j: (0, 0)),
        ] + c_in,
        out_specs=out_specs + c_out,
        out_shape=out_shape + c_shape,
        scratch_shapes=[pltpu.VMEM((tm, D_MODEL), BF16), pltpu.VMEM((n_s, D_MODEL), BF16)],
        compiler_params=_params(("arbitrary", "arbitrary")),
        name="in_proj",
    )(xp, xs, g, w, wx, bx, *[arr for arr, _ in casts])
    k = len(sections) + 1
    return outs[:k], outs[k:2 * k], outs[2 * k:]


def _out_proj_kernel(widths, xp_ref, xs_ref, w_ref, *refs):
    na = len(widths)
    ap, a_s = refs[:na], refs[na:2 * na]
    op_ref, os_ref = refs[2 * na:]
    i = pl.program_id(0)

    def compute(x_ref, a_refs, o_ref):
        acc = x_ref[...]
        off = 0
        for a_ref, kw in zip(a_refs, widths):
            acc = acc + _dot(a_ref[...].astype(BF16), w_ref[off:off + kw, :])
            off += kw
        o_ref[...] = acc

    compute(xp_ref, ap, op_ref)

    @pl.when(i == 0)
    def _():
        compute(xs_ref, a_s, os_ref)


def _out_proj(xp, xs, w, a_p, a_s, tm=SEQ_BLOCK, tn=D_MODEL):
    n_p, n_s = xp.shape[0], xs.shape[0]
    widths = tuple(a.shape[1] for a in a_p)
    ncol = D_MODEL // tn
    grid = (n_p // tm, ncol)
    s_col = lambda i, j: (0, jnp.where(i == 0, j, ncol - 1))
    in_specs = [
        pl.BlockSpec((tm, tn), lambda i, j: (i, j)),
        pl.BlockSpec((n_s, tn), s_col),
        pl.BlockSpec((D_MODEL, tn), lambda i, j: (0, j)),
    ]
    in_specs += [pl.BlockSpec((tm, kw), lambda i, j: (i, 0)) for kw in widths]
    in_specs += [pl.BlockSpec((n_s, kw), lambda i, j: (0, 0)) for kw in widths]
    return pl.pallas_call(
        functools.partial(_out_proj_kernel, widths),
        grid=grid,
        in_specs=in_specs,
        out_specs=[pl.BlockSpec((tm, tn), lambda i, j: (i, j)), pl.BlockSpec((n_s, tn), s_col)],
        out_shape=[jax.ShapeDtypeStruct((n_p, D_MODEL), F32), jax.ShapeDtypeStruct((n_s, D_MODEL), F32)],
        compiler_params=_params(("arbitrary", "arbitrary")),
        name="out_proj",
    )(xp, xs, w, *a_p, *a_s)


def _mlp_kernel(final_norm, n_cast, xp_ref, xs_ref, g_ref, gf_ref, wu_ref, wd_ref, *refs):
    cast_src, (op_ref, os_ref) = refs[:n_cast], refs[n_cast:n_cast + 2]
    cast_dst = refs[n_cast + 2:2 * n_cast + 2]
    xn_ref = refs[2 * n_cast + 2]
    n_s = xs_ref.shape[0]
    i = pl.program_id(0)
    j = pl.program_id(1)
    last = pl.num_programs(1) - 1
    _side_cast(cast_src, cast_dst)

    @pl.when(j == 0)
    def _():
        x = xp_ref[...]
        xn_ref[n_s:, :] = _rms_norm(x, g_ref[...]).astype(BF16)
        op_ref[...] = x

    @pl.when((j == 0) & (i == 0))
    def _():
        x = xs_ref[...]
        xn_ref[:n_s, :] = _rms_norm(x, g_ref[...]).astype(BF16)
        os_ref[...] = x

    def delta(xn):
        a = _dot(xn, wu_ref[...])
        return _dot(jnp.square(jnp.maximum(a, 0.0)).astype(BF16), wd_ref[...])

    @pl.when(i == 0)
    def _():
        d = delta(xn_ref[...])
        os_ref[...] += d[:n_s]
        op_ref[...] += d[n_s:]

    @pl.when(i != 0)
    def _():
        op_ref[...] += delta(xn_ref[n_s:, :])

    if final_norm:
        @pl.when(j == last)
        def _():
            op_ref[...] = _rms_norm(op_ref[...], gf_ref[...])

        @pl.when((j == last) & (i == 0))
        def _():
            os_ref[...] = _rms_norm(os_ref[...], gf_ref[...])


def _mlp(xp, xs, g, gf, wu, wd, final_norm, casts=(), drop_meta=None, tf=512):
    n_p, n_s = xp.shape[0], xs.shape[0]
    if drop_meta is None:
        tm, n_out = ROW_BLOCK, n_p
        x_spec = pl.BlockSpec((tm, D_MODEL), lambda i, j: (i, 0))
    else:
        seq, n_meta = drop_meta
        tm = (seq - n_meta) // 2
        n_out = n_p // seq * (seq - n_meta)
        x_spec = pl.BlockSpec((pl.Element(tm), pl.Element(D_MODEL)),
                              lambda i, j: (pl.multiple_of((i // 2) * seq + n_meta + (i % 2) * tm, 8), 0))
    grid = (n_out // tm, D_FF // tf)
    c_in, c_out, c_shape = _side_cast_specs(casts, grid[0] * grid[1], lambda i, j: i * grid[1] + j)
    outs = pl.pallas_call(
        functools.partial(_mlp_kernel, final_norm, len(casts)),
        grid=grid,
        in_specs=[
            x_spec,
            pl.BlockSpec((n_s, D_MODEL), lambda i, j: (0, 0)),
            pl.BlockSpec((1, D_MODEL), lambda i, j: (0, 0)),
            pl.BlockSpec((1, D_MODEL), lambda i, j: (0, 0)),
            pl.BlockSpec((D_MODEL, tf), lambda i, j: (0, j)),
            pl.BlockSpec((tf, D_MODEL), lambda i, j: (j, 0)),
        ] + c_in,
        out_specs=[pl.BlockSpec((tm, D_MODEL), lambda i, j: (i, 0)),
                   pl.BlockSpec((n_s, D_MODEL), lambda i, j: (0, 0))] + c_out,
        out_shape=[jax.ShapeDtypeStruct((n_out, D_MODEL), F32), jax.ShapeDtypeStruct((n_s, D_MODEL), F32)] + c_shape,
        scratch_shapes=[pltpu.VMEM((n_s + tm, D_MODEL), BF16)],
        compiler_params=_params(("arbitrary", "arbitrary")),
        name="mlp",
    )(xp, xs, g, gf, wu, wd, *[arr for arr, _ in casts])
    return outs[0], outs[1], outs[2:]


def _pool_groups(ext_ref, rows, count_fn, pw_ref, ps_ref, o_ref):
    for g, w in enumerate(POOL_WINDOWS):
        cs = slice(g * POOL_GROUP, (g + 1) * POOL_GROUP)
        u = ext_ref[16:16 + rows, cs]
        acc = u
        for back in range(1, w):
            acc = acc + ext_ref[16 - back:16 - back + rows, cs]
        pooled = acc / count_fn(w) - u
        mixed = _dot(pooled.astype(BF16), pw_ref[g]) * ps_ref[:, cs]
        o_ref[:, cs] = mixed.astype(o_ref.dtype)


def _pool_prompt_kernel(z_ref, pw_ref, ps_ref, o_ref, buf_ref, ext_ref):
    blk = pl.program_id(0) % (2064 // SEQ_BLOCK)

    @pl.when(blk == 0)
    def _():
        ext_ref[0:16, :] = jnp.zeros((16, D_POOL), F32)

    ext_ref[16:16 + SEQ_BLOCK, :] = z_ref[...]
    pos = blk * SEQ_BLOCK + lax.broadcasted_iota(jnp.int32, (SEQ_BLOCK, 1), 0)
    _pool_groups(ext_ref, SEQ_BLOCK, lambda w: jnp.minimum(w, pos + 1).astype(F32), pw_ref, ps_ref, o_ref)

    @pl.when(blk == 2064 // SEQ_BLOCK - 1)
    def _():
        buf_ref[...] = ext_ref[16 + SEQ_BLOCK - POOL_BUF:16 + SEQ_BLOCK, :]

    ext_ref[0:16, :] = ext_ref[SEQ_BLOCK:SEQ_BLOCK + 16, :]


def _pool_prompt(z_main, pw, ps, n_batch):
    n_p = z_main.shape[0]
    per_seq = 2064 // SEQ_BLOCK
    return pl.pallas_call(
        _pool_prompt_kernel,
        grid=(n_p // SEQ_BLOCK,),
        in_specs=[
            pl.BlockSpec((SEQ_BLOCK, D_POOL), lambda r: (r, 0)),
            pl.BlockSpec((len(POOL_WINDOWS), POOL_GROUP, POOL_GROUP), lambda r: (0, 0, 0)),
            pl.BlockSpec((1, D_POOL), lambda r: (0, 0)),
        ],
        out_specs=[pl.BlockSpec((SEQ_BLOCK, D_POOL), lambda r: (r, 0)),
                   pl.BlockSpec((None, None, POOL_BUF, D_POOL), lambda r: (0, r // per_seq, 0, 0))],
        out_shape=[jax.ShapeDtypeStruct((n_p, D_POOL), BF16),
                   jax.ShapeDtypeStruct((1, n_batch, POOL_BUF, D_POOL), F32)],
        scratch_shapes=[pltpu.VMEM((16 + SEQ_BLOCK, D_POOL), F32)],
        compiler_params=_params(("arbitrary",)),
        name="pool_prompt",
    )(z_main, pw, ps)


def _gla_chunk(qk, v, r, alr, wa2, ba, gnorm, st_get, st_set, o_set):
    c = qk.shape[0]
    x = _dot(alr.astype(BF16), wa2) + ba
    la = _log_sigmoid(x) * (1.0 / GLA_NORMALIZER)
    ri = lax.broadcasted_iota(jnp.int32, (c, c), 0)
    ci = lax.broadcasted_iota(jnp.int32, (c, c), 1)
    causal = ri >= ci
    b = _dot(causal.astype(F32), la, precision=HIGHEST)
    mid = c // 2 - 1
    bm = b[mid:mid + 1, :]
    be = b[c - 1:c, :]
    q = qk[:, :HK] * (GLA_DK ** -0.5)
    k = qk[:, HK:]
    qt = q * jnp.exp(b - bm)
    kt = k * jnp.exp(bm - b)
    qi = qt * jnp.exp(bm)
    khat = kt * jnp.exp(be - bm)
    e_end = jnp.exp(be)
    for h in range(GLA_HEADS):
        ks = slice(h * GLA_DK, (h + 1) * GLA_DK)
        vs = slice(h * GLA_DV, (h + 1) * GLA_DV)
        vh = v[:, vs].astype(BF16)
        att = _dot_nt(qt[:, ks].astype(BF16), kt[:, ks].astype(BF16))
        att = jnp.where(causal, att, 0.0)
        st = st_get(h)
        o = _dot(att.astype(BF16), vh) + _dot_nt(qi[:, ks].astype(BF16), st.astype(BF16))
        st_set(h, st * e_end[:, ks] + _dot_tn(vh, khat[:, ks].astype(BF16)))
        on = _rms_norm(o, gnorm[:, vs])
        rh = r[:, vs]
        gate = rh / (1.0 + jnp.exp(-rh))
        o_set(h, on * gate)


def _gla_prompt_kernel(n_batch, n_cast, qk_ref, v_ref, r_ref, alr_ref, wa2_ref, ba_ref, gn_ref, *refs):
    cast_src, (o_ref, s_ref) = refs[:n_cast], refs[n_cast:n_cast + 2]
    cast_dst, st_ref = refs[n_cast + 2:2 * n_cast + 2], refs[2 * n_cast + 2]
    c = pl.program_id(0)
    _side_cast(cast_src, cast_dst)

    @pl.when(c == 0)
    def _():
        st_ref[...] = jnp.zeros(st_ref.shape, F32)

    for bi in range(n_batch):
        def st_get(h, bi=bi):
            return st_ref[bi * GLA_HEADS + h]

        def st_set(h, val, bi=bi):
            st_ref[bi * GLA_HEADS + h] = val

        def o_set(h, val, bi=bi):
            o_ref[bi, :, h * GLA_DV:(h + 1) * GLA_DV] = val.astype(o_ref.dtype)

        _gla_chunk(qk_ref[bi], v_ref[bi], r_ref[bi], alr_ref[bi], wa2_ref[...], ba_ref[...], gn_ref[...],
                   st_get, st_set, o_set)

    @pl.when(c == pl.num_programs(0) - 1)
    def _():
        for bi in range(n_batch):
            for h in range(GLA_HEADS):
                s_ref[0, bi, h] = st_ref[bi * GLA_HEADS + h].T


def _gla_prompt(z_main3, zx3, wa2, ba, gn, casts=()):
    n_batch, seq = z_main3.shape[0], z_main3.shape[1]
    c_in, c_out, c_shape = _side_cast_specs(casts, seq // GLA_CHUNK, lambda c: c)
    blk = lambda col: pl.BlockSpec((n_batch, GLA_CHUNK, 1024), lambda c, col=col: (0, c, col))
    const2 = lambda shape: pl.BlockSpec(shape, lambda c: (0, 0))
    outs = pl.pallas_call(
        functools.partial(_gla_prompt_kernel, n_batch, len(casts)),
        grid=(seq // GLA_CHUNK,),
        in_specs=[blk(1), blk(2), blk(3),
                  pl.BlockSpec((n_batch, GLA_CHUNK, LANES), lambda c: (0, c, 0)),
                  const2((LANES, HK)), const2((1, HK)), const2((1, HV))] + c_in,
        out_specs=[pl.BlockSpec((n_batch, GLA_CHUNK, HV), lambda c: (0, c, 0)),
                   pl.BlockSpec((1, n_batch, GLA_HEADS, GLA_DK, GLA_DV), lambda c: (0, 0, 0, 0, 0))] + c_out,
        out_shape=[jax.ShapeDtypeStruct((n_batch, seq, HV), BF16),
                   jax.ShapeDtypeStruct((1, n_batch, GLA_HEADS, GLA_DK, GLA_DV), F32)] + c_shape,
        scratch_shapes=[pltpu.VMEM((n_batch * GLA_HEADS, GLA_DV, GLA_DK), F32)],
        compiler_params=_params(("arbitrary",)),
        name="gla_prompt",
    )(z_main3, z_main3, z_main3, zx3, wa2, ba, gn, *[arr for arr, _ in casts])
    return outs[0], outs[1], outs[2:]


def _even_sample_kernel(u_ref, qk_ref, v_ref, r_ref, alr_ref, pbuf_ref, s0_ref, pw_ref, ps_ref, wa2_ref, ba_ref,
                        gn_ref, po_ref, go_ref, nbuf_ref, ns_ref, ext_ref):
    t = u_ref.shape[0]
    ext_ref[0:1, :] = jnp.zeros((1, D_POOL), F32)
    ext_ref[1:16, :] = pbuf_ref[...]
    ext_ref[16:16 + t, :] = u_ref[...]
    _pool_groups(ext_ref, t, lambda w: float(w), pw_ref, ps_ref, po_ref)
    nbuf_ref[...] = ext_ref[16 + t - POOL_BUF:16 + t, :]

    def st_get(h):
        return s0_ref[h].T

    def st_set(h, val):
        ns_ref[h] = val.T

    def o_set(h, val):
        go_ref[:, h * GLA_DV:(h + 1) * GLA_DV] = val

    _gla_chunk(qk_ref[...], v_ref[...], r_ref[...], alr_ref[...], wa2_ref[...], ba_ref[...], gn_ref[...],
               st_get, st_set, o_set)


def _even_sample(zs_main, zxs, pool_buf, gla_s0, pw, ps, wa2, ba, gn, n_batch, t):
    blk = lambda col: pl.BlockSpec((t, 1024), lambda b, col=col: (b, col))
    const = lambda shape: pl.BlockSpec(shape, lambda b: (0,) * len(shape))
    return pl.pallas_call(
        _even_sample_kernel,
        grid=(n_batch,),
        in_specs=[blk(0), blk(1), blk(2), blk(3),
                  pl.BlockSpec((t, LANES), lambda b: (b, 0)),
                  pl.BlockSpec((None, None, POOL_BUF, D_POOL), lambda b: (0, b, 0, 0)),
                  pl.BlockSpec((None, None, GLA_HEADS, GLA_DK, GLA_DV), lambda b: (0, b, 0, 0, 0)),
                  const((len(POOL_WINDOWS), POOL_GROUP, POOL_GROUP)), const((1, D_POOL)),
                  const((LANES, HK)), const((1, HK)), const((1, HV))],
        out_specs=[pl.BlockSpec((t, D_POOL), lambda b: (b, 0)),
                   pl.BlockSpec((t, HV), lambda b: (b, 0)),
                   pl.BlockSpec((None, None, POOL_BUF, D_POOL), lambda b: (0, b, 0, 0)),
                   pl.BlockSpec((None, None, GLA_HEADS, GLA_DK, GLA_DV), lambda b: (0, b, 0, 0, 0))],
        out_shape=[jax.ShapeDtypeStruct((n_batch * t, D_POOL), F32),
                   jax.ShapeDtypeStruct((n_batch * t, HV), F32),
                   jax.ShapeDtypeStruct((1, n_batch, POOL_BUF, D_POOL), F32),
                   jax.ShapeDtypeStruct((1, n_batch, GLA_HEADS, GLA_DK, GLA_DV), F32)],
        scratch_shapes=[pltpu.VMEM((16 + t, D_POOL), F32)],
        compiler_params=_params(("arbitrary",)),
        name="even_sample",
    )(zs_main, zs_main, zs_main, zs_main, zxs, pool_buf, gla_s0, pw, ps, wa2, ba, gn)


HEADS_PER_STEP = 8
AUG_LANES = 8

def _split3(x):
    hi = x.astype(BF16)
    r1 = x - hi.astype(F32)
    mid = r1.astype(BF16)
    lo = (r1 - mid.astype(F32)).astype(BF16)
    return hi, mid, lo


def _forget_cumsum_kernel(lf_ref, augq_ref, augk_ref, a_ref, b_ref):
    seq = lf_ref.shape[0]
    pad = a_ref.shape[0] - seq
    a_ref[0:pad, :] = jnp.zeros((pad, LANES), F32)
    b_ref[0:pad, :] = jnp.zeros((pad, LANES), F32)
    a_ref[pad:, :] = lf_ref[...]
    src, dst = a_ref, b_ref
    shift = 1
    while shift < seq:
        dst[pad:, :] = src[pad:, :] + src[pad - shift:pad - shift + seq, :]
        src, dst = dst, src
        shift *= 2
    parts = _split3(src[pad:, :] * (FOX_HD ** 0.5))
    head = lax.broadcasted_iota(jnp.int32, (LANES, LANES), 0)
    dest = lax.broadcasted_iota(jnp.int32, (LANES, LANES), 1)
    lane = lax.broadcasted_iota(jnp.int32, (seq, LANES), 1)
    used = lane < HEADS_PER_STEP * AUG_LANES
    ones_q = (used & (lane % AUG_LANES >= 3) & (lane % AUG_LANES < 6)).astype(F32)
    ones_k = (used & (lane % AUG_LANES < 3)).astype(F32)
    for hg in range(FOX_HEADS // HEADS_PER_STEP):
        from_head = (head == hg * HEADS_PER_STEP + dest // AUG_LANES) & (dest < HEADS_PER_STEP * AUG_LANES)
        ft = sum(_dot(parts[c], (from_head & (dest % AUG_LANES == c)).astype(BF16)) for c in range(3))
        fs = sum(_dot(parts[c], (from_head & (dest % AUG_LANES == 3 + c)).astype(BF16)) for c in range(3))
        augq_ref[hg] = (ft + ones_q).astype(BF16)
        augk_ref[hg] = (ones_k - fs).astype(BF16)


def _forget_cumsum(lf3):
    n_batch, seq = lf3.shape[0], lf3.shape[1]
    pad = 2048
    n_groups = FOX_HEADS // HEADS_PER_STEP
    aug_spec = pl.BlockSpec((None, n_groups, seq, LANES), lambda b: (b, 0, 0, 0))
    aug_shape = jax.ShapeDtypeStruct((n_batch, n_groups, seq, LANES), BF16)
    return pl.pallas_call(
        _forget_cumsum_kernel,
        grid=(n_batch,),
        in_specs=[pl.BlockSpec((None, seq, LANES), lambda b: (b, 0, 0))],
        out_specs=[aug_spec, aug_spec],
        out_shape=[aug_shape, aug_shape],
        scratch_shapes=[pltpu.VMEM((pad + seq, LANES), F32), pltpu.VMEM((pad + seq, LANES), F32)],
        compiler_params=_params(("arbitrary",)),
        name="forget_cumsum",
    )(lf3)


def _fox_prompt_kernel(q_ref, k_ref, v_ref, augq_ref, augk_ref, o_ref, m_ref, acc_ref, qa_ref):
    qi = pl.program_id(2)
    ki = pl.program_id(3)
    tq = q_ref.shape[0]
    to_log2 = (FOX_HD ** -0.5) * 1.4426950408889634

    @pl.when(ki == 0)
    def _():
        m_ref[...] = jnp.full(m_ref.shape, NEG_BIG, F32)
        acc_ref[...] = jnp.zeros(acc_ref.shape, F32)
        lane = lax.broadcasted_iota(jnp.int32, (tq, LANES), 1)
        augq = augq_ref[...]
        for hh in range(HEADS_PER_STEP):
            cs = slice(hh * FOX_HD, (hh + 1) * FOX_HD)
            own = jnp.where(lane // AUG_LANES == hh, augq, jnp.zeros_like(augq))
            qa_ref[hh] = jnp.concatenate([q_ref[:, cs].astype(BF16), own], axis=1)

    def block(diagonal):
        augk = augk_ref[...]
        ones_col = (lax.broadcasted_iota(jnp.int32, (tq, LANES), 1) == 0).astype(BF16)
        if diagonal:
            visible = (lax.broadcasted_iota(jnp.int32, (tq, tq), 1) <= lax.broadcasted_iota(jnp.int32, (tq, tq), 0))

        for hh in range(HEADS_PER_STEP):
            cs = slice(hh * FOX_HD, (hh + 1) * FOX_HD)
            ka = jnp.concatenate([k_ref[:, cs].astype(BF16), augk], axis=1)
            s = _dot_nt(qa_ref[hh], ka) * to_log2
            if diagonal:
                s = jnp.where(visible, s, NEG_BIG)
            va = jnp.concatenate([v_ref[:, cs].astype(BF16), ones_col], axis=1)
            m_old = m_ref[hh]
            m_new = jnp.maximum(m_old, jnp.max(s, axis=-1, keepdims=True))
            p = jnp.exp2(s - m_new)
            acc_ref[hh] = jnp.exp2(m_old - m_new) * acc_ref[hh] + _dot(p.astype(BF16), va)
            m_ref[hh] = m_new

    @pl.when(ki < qi)
    def _():
        block(diagonal=False)

    @pl.when(ki == qi)
    def _():
        block(diagonal=True)
        for hh in range(HEADS_PER_STEP):
            acc = acc_ref[hh]
            o_ref[:, hh * FOX_HD:(hh + 1) * FOX_HD] = (acc[:, :FOX_HD] / acc[:, FOX_HD:FOX_HD + 1]).astype(o_ref.dtype)


def _fox_prompt(q3, k3, v3, augq, augk):
    n_batch, seq = q3.shape[0], q3.shape[1]
    nblk = seq // SEQ_BLOCK
    wcol = HEADS_PER_STEP * FOX_HD
    kv_spec = pl.BlockSpec((None, SEQ_BLOCK, wcol), lambda b, hg, qi, ki: (b, jnp.minimum(ki, qi), hg))
    return pl.pallas_call(
        _fox_prompt_kernel,
        grid=(n_batch, FOX_HEADS // HEADS_PER_STEP, nblk, nblk),
        in_specs=[
            pl.BlockSpec((None, SEQ_BLOCK, wcol), lambda b, hg, qi, ki: (b, qi, hg)),
            kv_spec, kv_spec,
            pl.BlockSpec((None, None, SEQ_BLOCK, LANES), lambda b, hg, qi, ki: (b, hg, qi, 0)),
            pl.BlockSpec((None, None, SEQ_BLOCK, LANES), lambda b, hg, qi, ki: (b, hg, jnp.minimum(ki, qi), 0)),
        ],
        out_specs=pl.BlockSpec((None, SEQ_BLOCK, wcol), lambda b, hg, qi, ki: (b, qi, hg)),
        out_shape=jax.ShapeDtypeStruct((n_batch, seq, D_MODEL), BF16),
        scratch_shapes=[pltpu.VMEM((HEADS_PER_STEP, SEQ_BLOCK, 1), F32),
                        pltpu.VMEM((HEADS_PER_STEP, SEQ_BLOCK, 2 * FOX_HD), F32),
                        pltpu.VMEM((HEADS_PER_STEP, SEQ_BLOCK, 2 * FOX_HD), BF16)],
        compiler_params=_params(("arbitrary", "arbitrary", "arbitrary", "arbitrary")),
        name="fox_prompt",
    )(q3, k3, v3, augq, augk)


HALF_HEADS = FOX_HEADS // 2
HALF_ROWS = PAGE_SIZE * HALF_HEADS


def _later_log_forget(x, carry):
    lane = lax.broadcasted_iota(jnp.int32, x.shape, 1)
    sub = lax.broadcasted_iota(jnp.int32, x.shape, 0)
    after = x
    before = x
    shift = HALF_HEADS
    while shift < LANES:
        after = after + jnp.where(lane + shift < LANES, pltpu.roll(after, LANES - shift, axis=1), 0.0)
        before = before + jnp.where(lane >= shift, pltpu.roll(before, shift, axis=1), 0.0)
        shift *= 2
    row_total = after + before - x
    below = row_total
    shift = 1
    while shift < x.shape[0]:
        below = below + jnp.where(sub + shift < x.shape[0], pltpu.roll(below, x.shape[0] - shift, axis=0), 0.0)
        shift *= 2
    later = (after - x) + (below - row_total) + carry
    return later, carry + below[0:1, :]


def _fox_sample_kernel(pt_ref, qs_ref, kn_ref, vn_ref, lfn_ref, *refs):
    g_pages = PAGES_PER_STEP
    k_pages = refs[:g_pages]
    v_pages = refs[g_pages:2 * g_pages]
    lf_pages = refs[2 * g_pages:3 * g_pages]
    o_ref = refs[3 * g_pages]
    q2_ref, acc_ref, m_ref, l_ref, lcol_ref, carry_ref = refs[3 * g_pages + 1:]
    j = pl.program_id(1)
    t = qs_ref.shape[0]
    rows = HALF_HEADS * t
    scale = FOX_HD ** -0.5
    row = lax.broadcasted_iota(jnp.int32, (rows, LANES), 0)
    lane = lax.broadcasted_iota(jnp.int32, (rows, LANES), 1)

    def online_update(hf, s, v):
        m_old = m_ref[hf]
        m_new = jnp.maximum(m_old, jnp.max(s, axis=-1, keepdims=True))
        alpha = jnp.exp(m_old - m_new)
        p = jnp.exp(s - m_new)
        l_ref[hf] = alpha * l_ref[hf] + jnp.sum(p, axis=-1, keepdims=True)
        acc_ref[hf] = alpha * acc_ref[hf] + _dot(p.astype(BF16), v)
        m_ref[hf] = m_new

    @pl.when(j == 0)
    def _():
        m_ref[...] = jnp.full(m_ref.shape, NEG_BIG, F32)
        l_ref[...] = jnp.zeros(l_ref.shape, F32)
        acc_ref[...] = jnp.zeros(acc_ref.shape, F32)
        carry_ref[...] = jnp.zeros(carry_ref.shape, F32)
        full_row = lax.broadcasted_iota(jnp.int32, (LANES, LANES), 0)
        full_lane = lax.broadcasted_iota(jnp.int32, (LANES, LANES), 1)
        lfn = jnp.where(lax.broadcasted_iota(jnp.int32, (t, LANES), 1) < FOX_HEADS, lfn_ref[...], 0.0)
        tri = (lax.broadcasted_iota(jnp.int32, (t, t), 0) >= lax.broadcasted_iota(jnp.int32, (t, t), 1))
        l_new = _dot(tri.astype(F32), lfn, precision=HIGHEST)
        l_pad = jnp.concatenate([l_new, jnp.zeros((LANES - t, LANES), F32)], axis=0)
        head_sel = (full_row // t == full_lane).astype(F32)
        l_t = _dot_nt(head_sel, l_pad, precision=HIGHEST)
        lcol = jnp.sum(jnp.where(full_lane == full_row % t, l_t, 0.0), axis=-1, keepdims=True)
        lrow = jnp.sum(jnp.where(full_lane == full_row, lcol, 0.0), axis=0, keepdims=True)
        lcol_ref[...] = lcol
        heads = range(FOX_HEADS)
        k_new = jnp.concatenate([kn_ref[:, h * FOX_HD:(h + 1) * FOX_HD] for h in heads], axis=0).astype(BF16)
        v_new = jnp.concatenate([vn_ref[:, h * FOX_HD:(h + 1) * FOX_HD] for h in heads], axis=0).astype(BF16)
        for hf in range(2):
            hs = range(hf * HALF_HEADS, (hf + 1) * HALF_HEADS)
            q2 = jnp.concatenate([qs_ref[:, h * FOX_HD:(h + 1) * FOX_HD] for h in hs], axis=0).astype(BF16)
            q2_ref[hf] = q2
            s = _dot_nt(q2, k_new) * scale + lcol[hf * rows:(hf + 1) * rows] - lrow
            visible = (lane // t == hf * HALF_HEADS + row // t) & (lane % t <= row % t)
            online_update(hf, jnp.where(visible, s, NEG_BIG), v_new)

    @pl.when(j > 0)
    def _():
        other_head = jnp.where(lane % HALF_HEADS == row // t, 0.0, NEG_BIG)
        for hf in range(2):
            hsl = slice(hf * HALF_HEADS, (hf + 1) * HALF_HEADS)
            carry = carry_ref[hf]
            bias, k_parts, v_parts = [], [], []
            for g in range(g_pages):
                k_half = k_pages[g][:, hsl, :].reshape(HALF_ROWS, FOX_HD).astype(BF16)
                v_half = v_pages[g][:, hsl, :].reshape(HALF_ROWS, FOX_HD).astype(BF16)
                k_parts.append(k_half)
                v_parts.append(v_half)
                later, carry = _later_log_forget(lf_pages[g][hf], carry)
                bias += [later[a:a + 1, :] + other_head for a in range(later.shape[0])]
            carry_ref[hf] = carry
            s = _dot_nt(q2_ref[hf], jnp.concatenate(k_parts, axis=0)) * scale + lcol_ref[hf * rows:(hf + 1) * rows, :]
            online_update(hf, s + jnp.concatenate(bias, axis=1), jnp.concatenate(v_parts, axis=0))

    @pl.when(j == pl.num_programs(1) - 1)
    def _():
        for hf in range(2):
            out = acc_ref[hf] / l_ref[hf]
            for hh in range(HALF_HEADS):
                h = hf * HALF_HEADS + hh
                o_ref[:, h * FOX_HD:(h + 1) * FOX_HD] = out[hh * t:(hh + 1) * t, :]


def _fox_sample(qs, ks, vs, lfs, cache_k, cache_v, cache_logf, page_table, n_batch, t):
    n_pages = page_table.shape[1]
    n_pool = cache_k.shape[1]
    g_pages = PAGES_PER_STEP
    steps = 1 + n_pages // g_pages
    pt_flat = page_table.reshape(-1)
    lf_halves = cache_logf[0].reshape(n_pool, PAGE_SIZE, 2, HALF_HEADS).transpose(0, 2, 1, 3)
    lf_halves = lf_halves.reshape(n_pool, 2, HALF_ROWS // LANES, LANES)

    def page_map(g, rank):
        def index_map(b, j, pt):
            idx = n_pages - 1 - (jnp.maximum(j, 1) - 1) * g_pages - g
            return (0,) * (rank - 4) + (pt[b * n_pages + idx], 0, 0, 0)
        return index_map

    row_spec = lambda w: pl.BlockSpec((t, w), lambda b, j, pt: (b, 0))
    in_specs = [row_spec(D_MODEL), row_spec(D_MODEL), row_spec(D_MODEL), row_spec(LANES)]
    kv_block = (None, None, PAGE_SIZE, FOX_HEADS, FOX_HD)
    in_specs += [pl.BlockSpec(kv_block, page_map(g, 5)) for g in range(g_pages)]
    in_specs += [pl.BlockSpec(kv_block, page_map(g, 5)) for g in range(g_pages)]
    in_specs += [pl.BlockSpec((None, 2, HALF_ROWS // LANES, LANES), page_map(g, 4)) for g in range(g_pages)]
    rows = HALF_HEADS * t
    grid_spec = pltpu.PrefetchScalarGridSpec(
        num_scalar_prefetch=1,
        grid=(n_batch, steps),
        in_specs=in_specs,
        out_specs=pl.BlockSpec((t, D_MODEL), lambda b, j, pt: (b, 0)),
        scratch_shapes=[pltpu.VMEM((2, rows, FOX_HD), BF16),
                        pltpu.VMEM((2, rows, FOX_HD), F32),
                        pltpu.VMEM((2, rows, 1), F32),
                        pltpu.VMEM((2, rows, 1), F32),
                        pltpu.VMEM((2 * rows, 1), F32),
                        pltpu.VMEM((2, 1, LANES), F32)],
    )
    return pl.pallas_call(
        _fox_sample_kernel,
        grid_spec=grid_spec,
        out_shape=jax.ShapeDtypeStruct((n_batch * t, D_MODEL), F32),
        compiler_params=_params(("arbitrary", "arbitrary")),
        name="fox_sample",
    )(pt_flat, qs, ks, vs, lfs, *([cache_k] * g_pages), *([cache_v] * g_pages), *([lf_halves] * g_pages))


def _pad_cols(w, width):
    return jnp.pad(w, ((0, 0), (0, width - w.shape[1])))


def kernel(x_prompt, x_sample, state_pool, state_gla, cache_k, cache_v, cache_logf, page_table, meta_tokens,
           norm_mix_e, w_in_e, pool_w, pool_scale, gla_w_a2, gla_b_a, gla_norm, w_out_e, norm_mix_o, w_in_o,
           fox_f_bias, w_out_o, norm_mlp, w_up, w_down, norm_final):
    n_bp, seq_p = x_prompt.shape[0], x_prompt.shape[1] + N_META
    n_bs, seq_s = x_sample.shape[0], x_sample.shape[1]
    meta = jnp.broadcast_to(meta_tokens[None].astype(x_prompt.dtype), (n_bp, N_META, D_MODEL))
    hp = jnp.concatenate([meta, x_prompt], axis=1).reshape(n_bp * seq_p, D_MODEL)
    hs = x_sample.reshape(n_bs * seq_s, D_MODEL)
    row = lambda v: v.reshape(1, -1)

    w_e = w_in_e[0].astype(BF16)
    n_main = D_POOL + 2 * HK + 2 * HV
    (zp, zxp), (zs, zxs), (w_out_e_bf,) = _in_proj(
        hp, hs, row(norm_mix_e[0]), w_e, _pad_cols(w_e[:, n_main:], LANES), jnp.zeros((1, LANES), F32),
        sections=(n_main // 1024,), tm=ROW_BLOCK, tn=1024, logsig_extra=False, casts=((w_out_e, 0),))
    pw = pool_w[0].astype(BF16)
    ps = row(pool_scale[0])
    wa2 = jnp.pad(gla_w_a2[0], ((0, LANES - GLA_RANK), (0, 0))).astype(BF16)
    ba = row(gla_b_a[0])
    gn = row(gla_norm[0])
    pool_p, pool_buf_p = _pool_prompt(zp, pw, ps, n_bp)
    gla_p, gla_state_p, (w_up0, w_down0) = _gla_prompt(
        zp.reshape(n_bp, seq_p, n_main), zxp.reshape(n_bp, seq_p, LANES), wa2, ba, gn, casts=((w_up, 0), (w_down, 0)))
    pool_s, gla_s, pool_buf_s, gla_state_s = _even_sample(zs, zxs, state_pool, state_gla, pw, ps, wa2, ba, gn,
                                                          n_bs, seq_s)
    hp, hs = _out_proj(hp, hs, w_out_e_bf, [pool_p, gla_p.reshape(n_bp * seq_p, HV)], [pool_s, gla_s])
    hp, hs, (w_up1, w_down1, w_out_o_bf) = _mlp(
        hp, hs, row(norm_mlp[0]), row(norm_final), w_up0, w_down0, final_norm=False,
        casts=((w_up, 1), (w_down, 1), (w_out_o, 0)))

    w_o = w_in_o[0].astype(BF16)
    fb = jnp.pad(row(fox_f_bias[0]), ((0, 0), (0, LANES - FOX_HEADS)))
    (qp, kp, vp, lfp), (qs, ks, vs, lfs), _ = _in_proj(
        hp, hs, row(norm_mix_o[0]), w_o, _pad_cols(w_o[:, 3 * D_MODEL:], LANES), fb,
        sections=(2, 2, 2), tm=SEQ_BLOCK, tn=1024, logsig_extra=True)
    shape3 = lambda a: a.reshape(n_bp, seq_p, a.shape[-1])
    augq, augk = _forget_cumsum(shape3(lfp))
    att_p = _fox_prompt(shape3(qp), shape3(kp), shape3(vp), augq, augk)
    att_s = _fox_sample(qs, ks, vs, lfs, cache_k, cache_v, cache_logf, page_table, n_bs, seq_s)
    hp, hs = _out_proj(hp, hs, w_out_o_bf, [att_p.reshape(n_bp * seq_p, D_MODEL)], [att_s])
    yp, ys, _ = _mlp(hp, hs, row(norm_mlp[1]), row(norm_final), w_up1, w_down1, final_norm=True,
                     drop_meta=(seq_p, N_META))

    y_prompt = yp.reshape(n_bp, seq_p - N_META, D_MODEL)
    y_sample = ys.reshape(n_bs, seq_s, D_MODEL)
    heads = lambda a, b, s: a.reshape(1, b, s, FOX_HEADS, FOX_HD)
    lf_out = lambda a, b, s: a[:, :FOX_HEADS].reshape(1, b, s, FOX_HEADS)
    return (y_prompt, y_sample, pool_buf_p, pool_buf_s, gla_state_p, gla_state_s,
            heads(kp, n_bp, seq_p), heads(ks, n_bs, seq_s), heads(vp, n_bp, seq_p), heads(vs, n_bs, seq_s),
            lf_out(lfp, n_bp, seq_p), lf_out(lfs, n_bs, seq_s))
```

```python
import functools

import jax
import jax.numpy as jnp
from jax import lax
from jax.experimental import pallas as pl
from jax.experimental.pallas import tpu as pltpu

F32 = jnp.float32
BF16 = jnp.bfloat16
HIGHEST = lax.Precision.HIGHEST

D_MODEL = 2048
N_META = 16
EPS = 1e-6
POOL_WINDOWS = (2, 4, 8, 16)
D_POOL = 1024
POOL_GROUP = 256
POOL_BUF = 15
GLA_HEADS = 4
GLA_DK = 128
GLA_DV = 256
GLA_RANK = 16
GLA_NORMALIZER = 16.0
FOX_HEADS = 16
FOX_HD = 128
PAGE_SIZE = 128
D_FF = 8192
HK = GLA_HEADS * GLA_DK
HV = GLA_HEADS * GLA_DV

LANES = 128
NEG_BIG = -1e30
VMEM_LIMIT = 56 * 1024 * 1024

ROW_BLOCK = 1032
SEQ_BLOCK = 688
GLA_CHUNK = 48
PAGES_PER_STEP = 8

NT_DIMS = (((1,), (1,)), ((), ()))
TN_DIMS = (((0,), (0,)), ((), ()))


def _params(semantics):
    return pltpu.CompilerParams(dimension_semantics=semantics, vmem_limit_bytes=VMEM_LIMIT)


def _dot(a, b, **kw):
    return jnp.dot(a, b, preferred_element_type=F32, **kw)


def _dot_nt(a, b, **kw):
    return lax.dot_general(a, b, NT_DIMS, preferred_element_type=F32, **kw)


def _dot_tn(a, b, **kw):
    return lax.dot_general(a, b, TN_DIMS, preferred_element_type=F32, **kw)


def _log_sigmoid(x):
    return jnp.minimum(x, 0.0) - jnp.log1p(jnp.exp(-jnp.abs(x)))


def _rms_norm(x, g):
    ms = jnp.mean(x * x, axis=-1, keepdims=True)
    return x * lax.rsqrt(ms + EPS) * g


BF16_ROWS = 16


def _side_cast_specs(srcs, n_steps, step_index):
    in_specs, out_specs, out_shape = [], [], []
    for arr, layer in srcs:
        rows, cols = arr.shape[1], arr.shape[2]
        blk = pl.cdiv(pl.cdiv(rows, n_steps), BF16_ROWS) * BF16_ROWS
        last = pl.cdiv(rows, blk) - 1
        in_specs.append(pl.BlockSpec(
            (None, blk, cols), lambda *g, layer=layer, last=last: (layer, jnp.minimum(step_index(*g), last), 0)))
        out_specs.append(pl.BlockSpec((blk, cols), lambda *g, last=last: (jnp.minimum(step_index(*g), last), 0)))
        out_shape.append(jax.ShapeDtypeStruct((rows, cols), BF16))
    return in_specs, out_specs, out_shape


def _meta_row_spec(tm, seq, width):
    bps = seq // tm
    return pl.BlockSpec(
        (pl.Element(tm), pl.Element(width)),
        lambda i, j: (pl.multiple_of((i // bps) * (seq - N_META) + jnp.maximum((i % bps) * tm - N_META, 0), 8), 0))


def _with_meta_prefix(x, meta_ref, is_first):
    shifted = jnp.concatenate([meta_ref[...], x[:x.shape[0] - N_META]], axis=0)
    return jnp.where(is_first, shifted, x)


def _side_cast(src_refs, dst_refs):
    for src_ref, dst_ref in zip(src_refs, dst_refs):
        dst_ref[...] = src_ref[...].astype(BF16)


def _in_proj_kernel(sections, logsig_extra, n_cast, bps, xp_ref, xs_ref, g_ref, w_ref, wx_ref, bx_ref, *refs):
    nsec = len(sections)
    if bps:
        meta_ref, refs = refs[0], refs[1:]
    cast_src, refs = refs[:n_cast], refs[n_cast:]
    outs_p, zxp_ref = refs[:nsec], refs[nsec]
    outs_s, zxs_ref = refs[nsec + 1:2 * nsec + 1], refs[2 * nsec + 1]
    cast_dst = refs[2 * nsec + 2:2 * nsec + 2 + n_cast]
    xnp_ref, xns_ref = refs[2 * nsec + 2 + n_cast:]
    i = pl.program_id(0)
    j = pl.program_id(1)
    _side_cast(cast_src, cast_dst)

    def extra(xn):
        e = _dot(xn, wx_ref[...]) + bx_ref[...]
        return _log_sigmoid(e) if logsig_extra else e

    @pl.when(j == 0)
    def _():
        x = xp_ref[...]
        if bps:
            x = _with_meta_prefix(x, meta_ref, i % bps == 0)
        xn = _rms_norm(x, g_ref[...]).astype(BF16)
        xnp_ref[...] = xn
        zxp_ref[...] = extra(xn)

    @pl.when((j == 0) & (i == 0))
    def _():
        xn = _rms_norm(xs_ref[...], g_ref[...]).astype(BF16)
        xns_ref[...] = xn
        zxs_ref[...] = extra(xn)

    def write(outs, x_ref):
        z = _dot(x_ref[...], w_ref[...])
        off = 0
        for o_ref, n in zip(outs, sections):
            if nsec == 1:
                o_ref[...] = z.astype(o_ref.dtype)
            else:
                @pl.when((j >= off) & (j < off + n))
                def _(o_ref=o_ref):
                    o_ref[...] = z.astype(o_ref.dtype)
            off += n

    write(outs_p, xnp_ref)

    @pl.when(i == 0)
    def _():
        write(outs_s, xns_ref)


def _in_proj(xp, xs, g, w, wx, bx, sections, tm, tn, logsig_extra, casts=(), meta=None, seq=None):
    n_p, n_s = xp.shape[0], xs.shape[0]
    if meta is not None:
        n_p = n_p // (seq - N_META) * seq
    ncol = sum(sections)
    grid = (n_p // tm, ncol)
    offs = [sum(sections[:k]) for k in range(len(sections))]

    def p_map(off, n):
        return lambda i, j: (i, jnp.clip(j - off, 0, n - 1))

    def s_map(off, n):
        return lambda i, j: (0, jnp.clip(jnp.where(i == 0, j, ncol - 1) - off, 0, n - 1))

    out_shape, out_specs = [], []
    for rows, tr, mk in ((n_p, tm, p_map), (n_s, n_s, s_map)):
        for off, n in zip(offs, sections):
            out_shape.append(jax.ShapeDtypeStruct((rows, n * tn), F32))
            out_specs.append(pl.BlockSpec((tr, tn), mk(off, n)))
        out_shape.append(jax.ShapeDtypeStruct((rows, LANES), F32))
        out_specs.append(pl.BlockSpec((tr, LANES), (lambda i, j: (i, 0)) if rows == n_p else (lambda i, j: (0, 0))))
    c_in, c_out, c_shape = _side_cast_specs(casts, grid[0] * grid[1], lambda i, j: i * ncol + j)
    outs = pl.pallas_call(
        functools.partial(_in_proj_kernel, tuple(sections), logsig_extra, len(casts), 0 if meta is None else seq // tm),
        grid=grid,
        in_specs=[
            pl.BlockSpec((tm, D_MODEL), lambda i, j: (i, 0)) if meta is None else _meta_row_spec(tm, seq, D_MODEL),
            pl.BlockSpec((n_s, D_MODEL), lambda i, j: (0, 0)),
            pl.BlockSpec((1, D_MODEL), lambda i, j: (0, 0)),
            pl.BlockSpec((D_MODEL, tn), lambda i, j: (0, j)),
            pl.BlockSpec((D_MODEL, LANES), lambda i, j: (0, 0)),
            pl.BlockSpec((1, LANES), lambda i, j: (0, 0)),
        ] + ([] if meta is None else [pl.BlockSpec((N_META, D_MODEL), lambda i, j: (0, 0))]) + c_in,
        out_specs=out_specs + c_out,
        out_shape=out_shape + c_shape,
        scratch_shapes=[pltpu.VMEM((tm, D_MODEL), BF16), pltpu.VMEM((n_s, D_MODEL), BF16)],
        compiler_params=_params(("arbitrary", "arbitrary")),
        name="in_proj",
    )(xp, xs, g, w, wx, bx, *([] if meta is None else [meta]), *[arr for arr, _ in casts])
    k = len(sections) + 1
    return outs[:k], outs[k:2 * k], outs[2 * k:]


def _out_proj_kernel(widths, bps, xp_ref, xs_ref, w_ref, *refs):
    na = len(widths)
    if bps:
        meta_ref, refs = refs[0], refs[1:]
    ap, a_s = refs[:na], refs[na:2 * na]
    op_ref, os_ref = refs[2 * na:]
    i = pl.program_id(0)

    def compute(x_ref, a_refs, o_ref, prompt=False):
        acc = x_ref[...]
        if prompt and bps:
            acc = _with_meta_prefix(acc, meta_ref, i % bps == 0)
        off = 0
        for a_ref, kw in zip(a_refs, widths):
            acc = acc + _dot(a_ref[...].astype(BF16), w_ref[off:off + kw, :])
            off += kw
        o_ref[...] = acc

    compute(xp_ref, ap, op_ref, prompt=True)

    @pl.when(i == 0)
    def _():
        compute(xs_ref, a_s, os_ref)


def _out_proj(xp, xs, w, a_p, a_s, tm=SEQ_BLOCK, tn=D_MODEL, meta=None, seq=None):
    n_p, n_s = a_p[0].shape[0], xs.shape[0]
    assert meta is None or tn == D_MODEL
    widths = tuple(a.shape[1] for a in a_p)
    ncol = D_MODEL // tn
    grid = (n_p // tm, ncol)
    s_col = lambda i, j: (0, jnp.where(i == 0, j, ncol - 1))
    in_specs = [
        pl.BlockSpec((tm, tn), lambda i, j: (i, j)) if meta is None else _meta_row_spec(tm, seq, tn),
        pl.BlockSpec((n_s, tn), s_col),
        pl.BlockSpec((D_MODEL, tn), lambda i, j: (0, j)),
    ]
    if meta is not None:
        in_specs.append(pl.BlockSpec((N_META, D_MODEL), lambda i, j: (0, 0)))
    in_specs += [pl.BlockSpec((tm, kw), lambda i, j: (i, 0)) for kw in widths]
    in_specs += [pl.BlockSpec((n_s, kw), lambda i, j: (0, 0)) for kw in widths]
    return pl.pallas_call(
        functools.partial(_out_proj_kernel, widths, 0 if meta is None else seq // tm),
        grid=grid,
        in_specs=in_specs,
        out_specs=[pl.BlockSpec((tm, tn), lambda i, j: (i, j)), pl.BlockSpec((n_s, tn), s_col)],
        out_shape=[jax.ShapeDtypeStruct((n_p, D_MODEL), F32), jax.ShapeDtypeStruct((n_s, D_MODEL), F32)],
        compiler_params=_params(("arbitrary", "arbitrary")),
        name="out_proj",
    )(xp, xs, w, *([] if meta is None else [meta]), *a_p, *a_s)


def _mlp_kernel(final_norm, n_cast, xp_ref, xs_ref, g_ref, gf_ref, wu_ref, wd_ref, *refs):
    cast_src, (op_ref, os_ref) = refs[:n_cast], refs[n_cast:n_cast + 2]
    cast_dst = refs[n_cast + 2:2 * n_cast + 2]
    xn_ref = refs[2 * n_cast + 2]
    n_s = xs_ref.shape[0]
    i = pl.program_id(0)
    j = pl.program_id(1)
    last = pl.num_programs(1) - 1
    _side_cast(cast_src, cast_dst)

    @pl.when(j == 0)
    def _():
        x = xp_ref[...]
        xn_ref[n_s:, :] = _rms_norm(x, g_ref[...]).astype(BF16)
        op_ref[...] = x

    @pl.when((j == 0) & (i == 0))
    def _():
        x = xs_ref[...]
        xn_ref[:n_s, :] = _rms_norm(x, g_ref[...]).astype(BF16)
        os_ref[...] = x

    def delta(xn):
        a = _dot(xn, wu_ref[...])
        return _dot(jnp.square(jnp.maximum(a, 0.0)).astype(BF16), wd_ref[...])

    @pl.when(i == 0)
    def _():
        d = delta(xn_ref[...])
        os_ref[...] += d[:n_s]
        op_ref[...] += d[n_s:]

    @pl.when(i != 0)
    def _():
        op_ref[...] += delta(xn_ref[n_s:, :])

    if final_norm:
        @pl.when(j == last)
        def _():
            op_ref[...] = _rms_norm(op_ref[...], gf_ref[...])

        @pl.when((j == last) & (i == 0))
        def _():
            os_ref[...] = _rms_norm(os_ref[...], gf_ref[...])


def _mlp(xp, xs, g, gf, wu, wd, final_norm, casts=(), drop_meta=None, tf=512):
    n_p, n_s = xp.shape[0], xs.shape[0]
    if drop_meta is None:
        tm, n_out = ROW_BLOCK, n_p
        x_spec = pl.BlockSpec((tm, D_MODEL), lambda i, j: (i, 0))
    else:
        seq, n_meta = drop_meta
        tm = (seq - n_meta) // 2
        n_out = n_p // seq * (seq - n_meta)
        x_spec = pl.BlockSpec((pl.Element(tm), pl.Element(D_MODEL)),
                              lambda i, j: (pl.multiple_of((i // 2) * seq + n_meta + (i % 2) * tm, 8), 0))
    grid = (n_out // tm, D_FF // tf)
    c_in, c_out, c_shape = _side_cast_specs(casts, grid[0] * grid[1], lambda i, j: i * grid[1] + j)
    outs = pl.pallas_call(
        functools.partial(_mlp_kernel, final_norm, len(casts)),
        grid=grid,
        in_specs=[
            x_spec,
            pl.BlockSpec((n_s, D_MODEL), lambda i, j: (0, 0)),
            pl.BlockSpec((1, D_MODEL), lambda i, j: (0, 0)),
            pl.BlockSpec((1, D_MODEL), lambda i, j: (0, 0)),
            pl.BlockSpec((D_MODEL, tf), lambda i, j: (0, j)),
            pl.BlockSpec((tf, D_MODEL), lambda i, j: (j, 0)),
        ] + c_in,
        out_specs=[pl.BlockSpec((tm, D_MODEL), lambda i, j: (i, 0)),
                   pl.BlockSpec((n_s, D_MODEL), lambda i, j: (0, 0))] + c_out,
        out_shape=[jax.ShapeDtypeStruct((n_out, D_MODEL), F32), jax.ShapeDtypeStruct((n_s, D_MODEL), F32)] + c_shape,
        scratch_shapes=[pltpu.VMEM((n_s + tm, D_MODEL), BF16)],
        compiler_params=_params(("arbitrary", "arbitrary")),
        name="mlp",
    )(xp, xs, g, gf, wu, wd, *[arr for arr, _ in casts])
    return outs[0], outs[1], outs[2:]


def _pool_groups(ext_ref, rows, count_fn, pw_ref, ps_ref, o_ref):
    for g, w in enumerate(POOL_WINDOWS):
        cs = slice(g * POOL_GROUP, (g + 1) * POOL_GROUP)
        u = ext_ref[16:16 + rows, cs]
        acc = u
        for back in range(1, w):
            acc = acc + ext_ref[16 - back:16 - back + rows, cs]
        pooled = acc / count_fn(w) - u
        mixed = _dot(pooled.astype(BF16), pw_ref[g]) * ps_ref[:, cs]
        o_ref[:, cs] = mixed.astype(o_ref.dtype)


def _pool_prompt_kernel(z_ref, pw_ref, ps_ref, o_ref, buf_ref, ext_ref):
    blk = pl.program_id(0) % (2064 // SEQ_BLOCK)

    @pl.when(blk == 0)
    def _():
        ext_ref[0:16, :] = jnp.zeros((16, D_POOL), F32)

    ext_ref[16:16 + SEQ_BLOCK, :] = z_ref[...]
    pos = blk * SEQ_BLOCK + lax.broadcasted_iota(jnp.int32, (SEQ_BLOCK, 1), 0)
    _pool_groups(ext_ref, SEQ_BLOCK, lambda w: jnp.minimum(w, pos + 1).astype(F32), pw_ref, ps_ref, o_ref)

    @pl.when(blk == 2064 // SEQ_BLOCK - 1)
    def _():
        buf_ref[...] = ext_ref[16 + SEQ_BLOCK - POOL_BUF:16 + SEQ_BLOCK, :]

    ext_ref[0:16, :] = ext_ref[SEQ_BLOCK:SEQ_BLOCK + 16, :]


def _pool_prompt(z_main, pw, ps, n_batch):
    n_p = z_main.shape[0]
    per_seq = 2064 // SEQ_BLOCK
    return pl.pallas_call(
        _pool_prompt_kernel,
        grid=(n_p // SEQ_BLOCK,),
        in_specs=[
            pl.BlockSpec((SEQ_BLOCK, D_POOL), lambda r: (r, 0)),
            pl.BlockSpec((len(POOL_WINDOWS), POOL_GROUP, POOL_GROUP), lambda r: (0, 0, 0)),
            pl.BlockSpec((1, D_POOL), lambda r: (0, 0)),
        ],
        out_specs=[pl.BlockSpec((SEQ_BLOCK, D_POOL), lambda r: (r, 0)),
                   pl.BlockSpec((None, None, POOL_BUF, D_POOL), lambda r: (0, r // per_seq, 0, 0))],
        out_shape=[jax.ShapeDtypeStruct((n_p, D_POOL), BF16),
                   jax.ShapeDtypeStruct((1, n_batch, POOL_BUF, D_POOL), F32)],
        scratch_shapes=[pltpu.VMEM((16 + SEQ_BLOCK, D_POOL), F32)],
        compiler_params=_params(("arbitrary",)),
        name="pool_prompt",
    )(z_main, pw, ps)


def _gla_chunk(qk, v, r, alr, wa2, ba, gnorm, st_get, st_set, o_set):
    c = qk.shape[0]
    x = _dot(alr.astype(BF16), wa2) + ba
    la = _log_sigmoid(x) * (1.0 / GLA_NORMALIZER)
    ri = lax.broadcasted_iota(jnp.int32, (c, c), 0)
    ci = lax.broadcasted_iota(jnp.int32, (c, c), 1)
    causal = ri >= ci
    b = _dot(causal.astype(F32), la, precision=HIGHEST)
    mid = c // 2 - 1
    bm = b[mid:mid + 1, :]
    be = b[c - 1:c, :]
    q = qk[:, :HK] * (GLA_DK ** -0.5)
    k = qk[:, HK:]
    qt = q * jnp.exp(b - bm)
    kt = k * jnp.exp(bm - b)
    qi = qt * jnp.exp(bm)
    khat = kt * jnp.exp(be - bm)
    e_end = jnp.exp(be)
    for h in range(GLA_HEADS):
        ks = slice(h * GLA_DK, (h + 1) * GLA_DK)
        vs = slice(h * GLA_DV, (h + 1) * GLA_DV)
        vh = v[:, vs].astype(BF16)
        att = _dot_nt(qt[:, ks].astype(BF16), kt[:, ks].astype(BF16))
        att = jnp.where(causal, att, 0.0)
        st = st_get(h)
        o = _dot(att.astype(BF16), vh) + _dot_nt(qi[:, ks].astype(BF16), st.astype(BF16))
        st_set(h, st * e_end[:, ks] + _dot_tn(vh, khat[:, ks].astype(BF16)))
        on = _rms_norm(o, gnorm[:, vs])
        rh = r[:, vs]
        gate = rh / (1.0 + jnp.exp(-rh))
        o_set(h, on * gate)


def _gla_prompt_kernel(n_batch, n_cast, qk_ref, v_ref, r_ref, alr_ref, wa2_ref, ba_ref, gn_ref, *refs):
    cast_src, (o_ref, s_ref) = refs[:n_cast], refs[n_cast:n_cast + 2]
    cast_dst, st_ref = refs[n_cast + 2:2 * n_cast + 2], refs[2 * n_cast + 2]
    c = pl.program_id(0)
    _side_cast(cast_src, cast_dst)

    @pl.when(c == 0)
    def _():
        st_ref[...] = jnp.zeros(st_ref.shape, F32)

    for bi in range(n_batch):
        def st_get(h, bi=bi):
            return st_ref[bi * GLA_HEADS + h]

        def st_set(h, val, bi=bi):
            st_ref[bi * GLA_HEADS + h] = val

        def o_set(h, val, bi=bi):
            o_ref[bi, :, h * GLA_DV:(h + 1) * GLA_DV] = val.astype(o_ref.dtype)

        _gla_chunk(qk_ref[bi], v_ref[bi], r_ref[bi], alr_ref[bi], wa2_ref[...], ba_ref[...], gn_ref[...],
                   st_get, st_set, o_set)

    @pl.when(c == pl.num_programs(0) - 1)
    def _():
        for bi in range(n_batch):
            for h in range(GLA_HEADS):
                s_ref[0, bi, h] = st_ref[bi * GLA_HEADS + h].T


def _gla_prompt(z_main3, zx3, wa2, ba, gn, casts=()):
    n_batch, seq = z_main3.shape[0], z_main3.shape[1]
    c_in, c_out, c_shape = _side_cast_specs(casts, seq // GLA_CHUNK, lambda c: c)
    blk = lambda col: pl.BlockSpec((n_batch, GLA_CHUNK, 1024), lambda c, col=col: (0, c, col))
    const2 = lambda shape: pl.BlockSpec(shape, lambda c: (0, 0))
    outs = pl.pallas_call(
        functools.partial(_gla_prompt_kernel, n_batch, len(casts)),
        grid=(seq // GLA_CHUNK,),
        in_specs=[blk(1), blk(2), blk(3),
                  pl.BlockSpec((n_batch, GLA_CHUNK, LANES), lambda c: (0, c, 0)),
                  const2((LANES, HK)), const2((1, HK)), const2((1, HV))] + c_in,
        out_specs=[pl.BlockSpec((n_batch, GLA_CHUNK, HV), lambda c: (0, c, 0)),
                   pl.BlockSpec((1, n_batch, GLA_HEADS, GLA_DK, GLA_DV), lambda c: (0, 0, 0, 0, 0))] + c_out,
        out_shape=[jax.ShapeDtypeStruct((n_batch, seq, HV), BF16),
                   jax.ShapeDtypeStruct((1, n_batch, GLA_HEADS, GLA_DK, GLA_DV), F32)] + c_shape,
        scratch_shapes=[pltpu.VMEM((n_batch * GLA_HEADS, GLA_DV, GLA_DK), F32)],
        compiler_params=_params(("arbitrary",)),
        name="gla_prompt",
    )(z_main3, z_main3, z_main3, zx3, wa2, ba, gn, *[arr for arr, _ in casts])
    return outs[0], outs[1], outs[2:]


def _even_sample_kernel(u_ref, qk_ref, v_ref, r_ref, alr_ref, pbuf_ref, s0_ref, pw_ref, ps_ref, wa2_ref, ba_ref,
                        gn_ref, po_ref, go_ref, nbuf_ref, ns_ref, ext_ref):
    t = u_ref.shape[0]
    ext_ref[0:1, :] = jnp.zeros((1, D_POOL), F32)
    ext_ref[1:16, :] = pbuf_ref[...]
    ext_ref[16:16 + t, :] = u_ref[...]
    _pool_groups(ext_ref, t, lambda w: float(w), pw_ref, ps_ref, po_ref)
    nbuf_ref[...] = ext_ref[16 + t - POOL_BUF:16 + t, :]

    def st_get(h):
        return s0_ref[h].T

    def st_set(h, val):
        ns_ref[h] = val.T

    def o_set(h, val):
        go_ref[:, h * GLA_DV:(h + 1) * GLA_DV] = val

    _gla_chunk(qk_ref[...], v_ref[...], r_ref[...], alr_ref[...], wa2_ref[...], ba_ref[...], gn_ref[...],
               st_get, st_set, o_set)


def _even_sample(zs_main, zxs, pool_buf, gla_s0, pw, ps, wa2, ba, gn, n_batch, t):
    blk = lambda col: pl.BlockSpec((t, 1024), lambda b, col=col: (b, col))
    const = lambda shape: pl.BlockSpec(shape, lambda b: (0,) * len(shape))
    return pl.pallas_call(
        _even_sample_kernel,
        grid=(n_batch,),
        in_specs=[blk(0), blk(1), blk(2), blk(3),
                  pl.BlockSpec((t, LANES), lambda b: (b, 0)),
                  pl.BlockSpec((None, None, POOL_BUF, D_POOL), lambda b: (0, b, 0, 0)),
                  pl.BlockSpec((None, None, GLA_HEADS, GLA_DK, GLA_DV), lambda b: (0, b, 0, 0, 0)),
                  const((len(POOL_WINDOWS), POOL_GROUP, POOL_GROUP)), const((1, D_POOL)),
                  const((LANES, HK)), const((1, HK)), const((1, HV))],
        out_specs=[pl.BlockSpec((t, D_POOL), lambda b: (b, 0)),
                   pl.BlockSpec((t, HV), lambda b: (b, 0)),
                   pl.BlockSpec((None, None, POOL_BUF, D_POOL), lambda b: (0, b, 0, 0)),
                   pl.BlockSpec((None, None, GLA_HEADS, GLA_DK, GLA_DV), lambda b: (0, b, 0, 0, 0))],
        out_shape=[jax.ShapeDtypeStruct((n_batch * t, D_POOL), F32),
                   jax.ShapeDtypeStruct((n_batch * t, HV), F32),
                   jax.ShapeDtypeStruct((1, n_batch, POOL_BUF, D_POOL), F32),
                   jax.ShapeDtypeStruct((1, n_batch, GLA_HEADS, GLA_DK, GLA_DV), F32)],
        scratch_shapes=[pltpu.VMEM((16 + t, D_POOL), F32)],
        compiler_params=_params(("arbitrary",)),
        name="even_sample",
    )(zs_main, zs_main, zs_main, zs_main, zxs, pool_buf, gla_s0, pw, ps, wa2, ba, gn)


HEADS_PER_STEP = 8
AUG_LANES = 8

def _split3(x):
    hi = x.astype(BF16)
    r1 = x - hi.astype(F32)
    mid = r1.astype(BF16)
    lo = (r1 - mid.astype(F32)).astype(BF16)
    return hi, mid, lo


def _forget_cumsum_kernel(lf_ref, augq_ref, augk_ref, a_ref, b_ref):
    seq = lf_ref.shape[0]
    pad = a_ref.shape[0] - seq
    a_ref[0:pad, :] = jnp.zeros((pad, LANES), F32)
    b_ref[0:pad, :] = jnp.zeros((pad, LANES), F32)
    a_ref[pad:, :] = lf_ref[...]
    src, dst = a_ref, b_ref
    shift = 1
    while shift < seq:
        dst[pad:, :] = src[pad:, :] + src[pad - shift:pad - shift + seq, :]
        src, dst = dst, src
        shift *= 2
    parts = _split3(src[pad:, :] * (FOX_HD ** 0.5))
    head = lax.broadcasted_iota(jnp.int32, (LANES, LANES), 0)
    dest = lax.broadcasted_iota(jnp.int32, (LANES, LANES), 1)
    lane = lax.broadcasted_iota(jnp.int32, (seq, LANES), 1)
    used = lane < HEADS_PER_STEP * AUG_LANES
    ones_q = (used & (lane % AUG_LANES >= 3) & (lane % AUG_LANES < 6)).astype(F32)
    ones_k = (used & (lane % AUG_LANES < 3)).astype(F32)
    for hg in range(FOX_HEADS // HEADS_PER_STEP):
        from_head = (head == hg * HEADS_PER_STEP + dest // AUG_LANES) & (dest < HEADS_PER_STEP * AUG_LANES)
        ft = sum(_dot(parts[c], (from_head & (dest % AUG_LANES == c)).astype(BF16)) for c in range(3))
        fs = sum(_dot(parts[c], (from_head & (dest % AUG_LANES == 3 + c)).astype(BF16)) for c in range(3))
        augq_ref[hg] = (ft + ones_q).astype(BF16)
        augk_ref[hg] = (ones_k - fs).astype(BF16)


def _forget_cumsum(lf3):
    n_batch, seq = lf3.shape[0], lf3.shape[1]
    pad = 2048
    n_groups = FOX_HEADS // HEADS_PER_STEP
    aug_spec = pl.BlockSpec((None, n_groups, seq, LANES), lambda b: (b, 0, 0, 0))
    aug_shape = jax.ShapeDtypeStruct((n_batch, n_groups, seq, LANES), BF16)
    return pl.pallas_call(
        _forget_cumsum_kernel,
        grid=(n_batch,),
        in_specs=[pl.BlockSpec((None, seq, LANES), lambda b: (b, 0, 0))],
        out_specs=[aug_spec, aug_spec],
        out_shape=[aug_shape, aug_shape],
        scratch_shapes=[pltpu.VMEM((pad + seq, LANES), F32), pltpu.VMEM((pad + seq, LANES), F32)],
        compiler_params=_params(("arbitrary",)),
        name="forget_cumsum",
    )(lf3)


def _fox_prompt_kernel(q_ref, k_ref, v_ref, augq_ref, augk_ref, o_ref, m_ref, acc_ref, qa_ref):
    qi = pl.program_id(2)
    ki = pl.program_id(3)
    tq = q_ref.shape[0]
    to_log2 = (FOX_HD ** -0.5) * 1.4426950408889634

    @pl.when(ki == 0)
    def _():
        m_ref[...] = jnp.full(m_ref.shape, NEG_BIG, F32)
        acc_ref[...] = jnp.zeros(acc_ref.shape, F32)
        lane = lax.broadcasted_iota(jnp.int32, (tq, LANES), 1)
        augq = augq_ref[...]
        for hh in range(HEADS_PER_STEP):
            cs = slice(hh * FOX_HD, (hh + 1) * FOX_HD)
            own = jnp.where(lane // AUG_LANES == hh, augq, jnp.zeros_like(augq))
            qa_ref[hh] = jnp.concatenate([q_ref[:, cs].astype(BF16), own], axis=1)

    def block(diagonal):
        augk = augk_ref[...]
        ones_col = (lax.broadcasted_iota(jnp.int32, (tq, LANES), 1) == 0).astype(BF16)
        if diagonal:
            visible = (lax.broadcasted_iota(jnp.int32, (tq, tq), 1) <= lax.broadcasted_iota(jnp.int32, (tq, tq), 0))

        for hh in range(HEADS_PER_STEP):
            cs = slice(hh * FOX_HD, (hh + 1) * FOX_HD)
            ka = jnp.concatenate([k_ref[:, cs].astype(BF16), augk], axis=1)
            s = _dot_nt(qa_ref[hh], ka) * to_log2
            if diagonal:
                s = jnp.where(visible, s, NEG_BIG)
            va = jnp.concatenate([v_ref[:, cs].astype(BF16), ones_col], axis=1)
            m_old = m_ref[hh]
            m_new = jnp.maximum(m_old, jnp.max(s, axis=-1, keepdims=True))
            p = jnp.exp2(s - m_new)
            acc_ref[hh] = jnp.exp2(m_old - m_new) * acc_ref[hh] + _dot(p.astype(BF16), va)
            m_ref[hh] = m_new

    @pl.when(ki < qi)
    def _():
        block(diagonal=False)

    @pl.when(ki == qi)
    def _():
        block(diagonal=True)
        for hh in range(HEADS_PER_STEP):
            acc = acc_ref[hh]
            o_ref[:, hh * FOX_HD:(hh + 1) * FOX_HD] = (acc[:, :FOX_HD] / acc[:, FOX_HD:FOX_HD + 1]).astype(o_ref.dtype)


def _fox_prompt(q3, k3, v3, augq, augk):
    n_batch, seq = q3.shape[0], q3.shape[1]
    nblk = seq // SEQ_BLOCK
    wcol = HEADS_PER_STEP * FOX_HD
    kv_spec = pl.BlockSpec((None, SEQ_BLOCK, wcol), lambda b, hg, qi, ki: (b, jnp.minimum(ki, qi), hg))
    return pl.pallas_call(
        _fox_prompt_kernel,
        grid=(n_batch, FOX_HEADS // HEADS_PER_STEP, nblk, nblk),
        in_specs=[
            pl.BlockSpec((None, SEQ_BLOCK, wcol), lambda b, hg, qi, ki: (b, qi, hg)),
            kv_spec, kv_spec,
            pl.BlockSpec((None, None, SEQ_BLOCK, LANES), lambda b, hg, qi, ki: (b, hg, qi, 0)),
            pl.BlockSpec((None, None, SEQ_BLOCK, LANES), lambda b, hg, qi, ki: (b, hg, jnp.minimum(ki, qi), 0)),
        ],
        out_specs=pl.BlockSpec((None, SEQ_BLOCK, wcol), lambda b, hg, qi, ki: (b, qi, hg)),
        out_shape=jax.ShapeDtypeStruct((n_batch, seq, D_MODEL), BF16),
        scratch_shapes=[pltpu.VMEM((HEADS_PER_STEP, SEQ_BLOCK, 1), F32),
                        pltpu.VMEM((HEADS_PER_STEP, SEQ_BLOCK, 2 * FOX_HD), F32),
                        pltpu.VMEM((HEADS_PER_STEP, SEQ_BLOCK, 2 * FOX_HD), BF16)],
        compiler_params=_params(("arbitrary", "arbitrary", "arbitrary", "arbitrary")),
        name="fox_prompt",
    )(q3, k3, v3, augq, augk)


HALF_HEADS = FOX_HEADS // 2
HALF_ROWS = PAGE_SIZE * HALF_HEADS


def _later_log_forget(x, carry):
    lane = lax.broadcasted_iota(jnp.int32, x.shape, 1)
    sub = lax.broadcasted_iota(jnp.int32, x.shape, 0)
    after = x
    before = x
    shift = HALF_HEADS
    while shift < LANES:
        after = after + jnp.where(lane + shift < LANES, pltpu.roll(after, LANES - shift, axis=1), 0.0)
        before = before + jnp.where(lane >= shift, pltpu.roll(before, shift, axis=1), 0.0)
        shift *= 2
    row_total = after + before - x
    below = row_total
    shift = 1
    while shift < x.shape[0]:
        below = below + jnp.where(sub + shift < x.shape[0], pltpu.roll(below, x.shape[0] - shift, axis=0), 0.0)
        shift *= 2
    later = (after - x) + (below - row_total) + carry
    return later, carry + below[0:1, :]


def _fox_sample_kernel(pt_ref, qs_ref, kn_ref, vn_ref, lfn_ref, *refs):
    g_pages = PAGES_PER_STEP
    k_pages = refs[:g_pages]
    v_pages = refs[g_pages:2 * g_pages]
    lf_pages = refs[2 * g_pages:3 * g_pages]
    o_ref = refs[3 * g_pages]
    q2_ref, acc_ref, m_ref, l_ref, lcol_ref, carry_ref = refs[3 * g_pages + 1:]
    j = pl.program_id(1)
    t = qs_ref.shape[0]
    rows = HALF_HEADS * t
    scale = FOX_HD ** -0.5
    row = lax.broadcasted_iota(jnp.int32, (rows, LANES), 0)
    lane = lax.broadcasted_iota(jnp.int32, (rows, LANES), 1)

    def online_update(hf, s, v):
        m_old = m_ref[hf]
        m_new = jnp.maximum(m_old, jnp.max(s, axis=-1, keepdims=True))
        alpha = jnp.exp(m_old - m_new)
        p = jnp.exp(s - m_new)
        l_ref[hf] = alpha * l_ref[hf] + jnp.sum(p, axis=-1, keepdims=True)
        acc_ref[hf] = alpha * acc_ref[hf] + _dot(p.astype(BF16), v)
        m_ref[hf] = m_new

    @pl.when(j == 0)
    def _():
        m_ref[...] = jnp.full(m_ref.shape, NEG_BIG, F32)
        l_ref[...] = jnp.zeros(l_ref.shape, F32)
        acc_ref[...] = jnp.zeros(acc_ref.shape, F32)
        carry_ref[...] = jnp.zeros(carry_ref.shape, F32)
        full_row = lax.broadcasted_iota(jnp.int32, (LANES, LANES), 0)
        full_lane = lax.broadcasted_iota(jnp.int32, (LANES, LANES), 1)
        lfn = jnp.where(lax.broadcasted_iota(jnp.int32, (t, LANES), 1) < FOX_HEADS, lfn_ref[...], 0.0)
        tri = (lax.broadcasted_iota(jnp.int32, (t, t), 0) >= lax.broadcasted_iota(jnp.int32, (t, t), 1))
        l_new = _dot(tri.astype(F32), lfn, precision=HIGHEST)
        l_pad = jnp.concatenate([l_new, jnp.zeros((LANES - t, LANES), F32)], axis=0)
        head_sel = (full_row // t == full_lane).astype(F32)
        l_t = _dot_nt(head_sel, l_pad, precision=HIGHEST)
        lcol = jnp.sum(jnp.where(full_lane == full_row % t, l_t, 0.0), axis=-1, keepdims=True)
        lrow = jnp.sum(jnp.where(full_lane == full_row, lcol, 0.0), axis=0, keepdims=True)
        lcol_ref[...] = lcol
        heads = range(FOX_HEADS)
        k_new = jnp.concatenate([kn_ref[:, h * FOX_HD:(h + 1) * FOX_HD] for h in heads], axis=0).astype(BF16)
        v_new = jnp.concatenate([vn_ref[:, h * FOX_HD:(h + 1) * FOX_HD] for h in heads], axis=0).astype(BF16)
        for hf in range(2):
            hs = range(hf * HALF_HEADS, (hf + 1) * HALF_HEADS)
            q2 = jnp.concatenate([qs_ref[:, h * FOX_HD:(h + 1) * FOX_HD] for h in hs], axis=0).astype(BF16)
            q2_ref[hf] = q2
            s = _dot_nt(q2, k_new) * scale + lcol[hf * rows:(hf + 1) * rows] - lrow
            visible = (lane // t == hf * HALF_HEADS + row // t) & (lane % t <= row % t)
            online_update(hf, jnp.where(visible, s, NEG_BIG), v_new)

    @pl.when(j > 0)
    def _():
        other_head = jnp.where(lane % HALF_HEADS == row // t, 0.0, NEG_BIG)
        for hf in range(2):
            hsl = slice(hf * HALF_HEADS, (hf + 1) * HALF_HEADS)
            carry = carry_ref[hf]
            bias, k_parts, v_parts = [], [], []
            for g in range(g_pages):
                k_half = k_pages[g][:, hsl, :].reshape(HALF_ROWS, FOX_HD).astype(BF16)
                v_half = v_pages[g][:, hsl, :].reshape(HALF_ROWS, FOX_HD).astype(BF16)
                k_parts.append(k_half)
                v_parts.append(v_half)
                later, carry = _later_log_forget(lf_pages[g][hf], carry)
                bias += [later[a:a + 1, :] + other_head for a in range(later.shape[0])]
            carry_ref[hf] = carry
            s = _dot_nt(q2_ref[hf], jnp.concatenate(k_parts, axis=0)) * scale + lcol_ref[hf * rows:(hf + 1) * rows, :]
            online_update(hf, s + jnp.concatenate(bias, axis=1), jnp.concatenate(v_parts, axis=0))

    @pl.when(j == pl.num_programs(1) - 1)
    def _():
        for hf in range(2):
            out = acc_ref[hf] / l_ref[hf]
            for hh in range(HALF_HEADS):
                h = hf * HALF_HEADS + hh
                o_ref[:, h * FOX_HD:(h + 1) * FOX_HD] = out[hh * t:(hh + 1) * t, :]


def _fox_sample(qs, ks, vs, lfs, cache_k, cache_v, cache_logf, page_table, n_batch, t):
    n_pages = page_table.shape[1]
    n_pool = cache_k.shape[1]
    g_pages = PAGES_PER_STEP
    steps = 1 + n_pages // g_pages
    pt_flat = page_table.reshape(-1)
    lf_halves = cache_logf[0].reshape(n_pool, PAGE_SIZE, 2, HALF_HEADS).transpose(0, 2, 1, 3)
    lf_halves = lf_halves.reshape(n_pool, 2, HALF_ROWS // LANES, LANES)

    def page_map(g, rank):
        def index_map(b, j, pt):
            idx = n_pages - 1 - (jnp.maximum(j, 1) - 1) * g_pages - g
            return (0,) * (rank - 4) + (pt[b * n_pages + idx], 0, 0, 0)
        return index_map

    row_spec = lambda w: pl.BlockSpec((t, w), lambda b, j, pt: (b, 0))
    in_specs = [row_spec(D_MODEL), row_spec(D_MODEL), row_spec(D_MODEL), row_spec(LANES)]
    kv_block = (None, None, PAGE_SIZE, FOX_HEADS, FOX_HD)
    in_specs += [pl.BlockSpec(kv_block, page_map(g, 5)) for g in range(g_pages)]
    in_specs += [pl.BlockSpec(kv_block, page_map(g, 5)) for g in range(g_pages)]
    in_specs += [pl.BlockSpec((None, 2, HALF_ROWS // LANES, LANES), page_map(g, 4)) for g in range(g_pages)]
    rows = HALF_HEADS * t
    grid_spec = pltpu.PrefetchScalarGridSpec(
        num_scalar_prefetch=1,
        grid=(n_batch, steps),
        in_specs=in_specs,
        out_specs=pl.BlockSpec((t, D_MODEL), lambda b, j, pt: (b, 0)),
        scratch_shapes=[pltpu.VMEM((2, rows, FOX_HD), BF16),
                        pltpu.VMEM((2, rows, FOX_HD), F32),
                        pltpu.VMEM((2, rows, 1), F32),
                        pltpu.VMEM((2, rows, 1), F32),
                        pltpu.VMEM((2 * rows, 1), F32),
                        pltpu.VMEM((2, 1, LANES), F32)],
    )
    return pl.pallas_call(
        _fox_sample_kernel,
        grid_spec=grid_spec,
        out_shape=jax.ShapeDtypeStruct((n_batch * t, D_MODEL), F32),
        compiler_params=_params(("arbitrary", "arbitrary")),
        name="fox_sample",
    )(pt_flat, qs, ks, vs, lfs, *([cache_k] * g_pages), *([cache_v] * g_pages), *([lf_halves] * g_pages))


def _pad_cols(w, width):
    return jnp.pad(w, ((0, 0), (0, width - w.shape[1])))


def kernel(x_prompt, x_sample, state_pool, state_gla, cache_k, cache_v, cache_logf, page_table, meta_tokens,
           norm_mix_e, w_in_e, pool_w, pool_scale, gla_w_a2, gla_b_a, gla_norm, w_out_e, norm_mix_o, w_in_o,
           fox_f_bias, w_out_o, norm_mlp, w_up, w_down, norm_final):
    n_bp, seq_p = x_prompt.shape[0], x_prompt.shape[1] + N_META
    n_bs, seq_s = x_sample.shape[0], x_sample.shape[1]
    meta = meta_tokens.astype(x_prompt.dtype)
    xp = x_prompt.reshape(n_bp * (seq_p - N_META), D_MODEL)
    hs = x_sample.reshape(n_bs * seq_s, D_MODEL)
    row = lambda v: v.reshape(1, -1)

    w_e = w_in_e[0].astype(BF16)
    n_main = D_POOL + 2 * HK + 2 * HV
    (zp, zxp), (zs, zxs), (w_out_e_bf,) = _in_proj(
        xp, hs, row(norm_mix_e[0]), w_e, _pad_cols(w_e[:, n_main:], LANES), jnp.zeros((1, LANES), F32),
        sections=(n_main // 1024,), tm=ROW_BLOCK, tn=1024, logsig_extra=False, casts=((w_out_e, 0),),
        meta=meta, seq=seq_p)
    pw = pool_w[0].astype(BF16)
    ps = row(pool_scale[0])
    wa2 = jnp.pad(gla_w_a2[0], ((0, LANES - GLA_RANK), (0, 0))).astype(BF16)
    ba = row(gla_b_a[0])
    gn = row(gla_norm[0])
    pool_p, pool_buf_p = _pool_prompt(zp, pw, ps, n_bp)
    gla_p, gla_state_p, (w_up0, w_down0) = _gla_prompt(
        zp.reshape(n_bp, seq_p, n_main), zxp.reshape(n_bp, seq_p, LANES), wa2, ba, gn, casts=((w_up, 0), (w_down, 0)))
    pool_s, gla_s, pool_buf_s, gla_state_s = _even_sample(zs, zxs, state_pool, state_gla, pw, ps, wa2, ba, gn,
                                                          n_bs, seq_s)
    hp, hs = _out_proj(xp, hs, w_out_e_bf, [pool_p, gla_p.reshape(n_bp * seq_p, HV)], [pool_s, gla_s],
                       meta=meta, seq=seq_p)
    hp, hs, (w_up1, w_down1, w_out_o_bf) = _mlp(
        hp, hs, row(norm_mlp[0]), row(norm_final), w_up0, w_down0, final_norm=False,
        casts=((w_up, 1), (w_down, 1), (w_out_o, 0)))

    w_o = w_in_o[0].astype(BF16)
    fb = jnp.pad(row(fox_f_bias[0]), ((0, 0), (0, LANES - FOX_HEADS)))
    (qp, kp, vp, lfp), (qs, ks, vs, lfs), _ = _in_proj(
        hp, hs, row(norm_mix_o[0]), w_o, _pad_cols(w_o[:, 3 * D_MODEL:], LANES), fb,
        sections=(2, 2, 2), tm=SEQ_BLOCK, tn=1024, logsig_extra=True)
    shape3 = lambda a: a.reshape(n_bp, seq_p, a.shape[-1])
    augq, augk = _forget_cumsum(shape3(lfp))
    att_p = _fox_prompt(shape3(qp), shape3(kp), shape3(vp), augq, augk)
    att_s = _fox_sample(qs, ks, vs, lfs, cache_k, cache_v, cache_logf, page_table, n_bs, seq_s)
    hp, hs = _out_proj(hp, hs, w_out_o_bf, [att_p.reshape(n_bp * seq_p, D_MODEL)], [att_s])
    yp, ys, _ = _mlp(hp, hs, row(norm_mlp[1]), row(norm_final), w_up1, w_down1, final_norm=True,
                     drop_meta=(seq_p, N_META))

    y_prompt = yp.reshape(n_bp, seq_p - N_META, D_MODEL)
    y_sample = ys.reshape(n_bs, seq_s, D_MODEL)
    heads = lambda a, b, s: a.reshape(1, b, s, FOX_HEADS, FOX_HD)
    lf_out = lambda a, b, s: a[:, :FOX_HEADS].reshape(1, b, s, FOX_HEADS)
    return (y_prompt, y_sample, pool_buf_p, pool_buf_s, gla_state_p, gla_state_s,
            heads(kp, n_bp, seq_p), heads(ks, n_bs, seq_s), heads(vp, n_bp, seq_p), heads(vs, n_bs, seq_s),
            lf_out(lfp, n_bp, seq_p), lf_out(lfs, n_bs, seq_s))
```
